```python
import math
import jax, jax.numpy as jnp
from jax import lax
import numpy as np

D_MODEL = 4096
BATCH = 2
SEQ = 4096
DEPTH = 2

HEAD_DIM = 128
DIFF_HEADS = 6
DIFF_DV = 2 * HEAD_DIM
MOBA_HEADS = 10
DSA_HEADS = 10
MIX_WIDTH = DIFF_HEADS * DIFF_DV + MOBA_HEADS * HEAD_DIM + DSA_HEADS * HEAD_DIM
MOBA_BLOCK = 256
MOBA_TOPK = 3
MOBA_Q_CHUNK = 32
DSA_TOPK = 256
KV_LATENT = 512
IDX_HEADS = 32
IDX_DIM = 64
DSA_Q_CHUNK = 128
DENSE_Q_BLOCK = 128
NUM_BUCKETS = 32
MAX_DISTANCE = 128
N_BIAS_COLS = 2 * DIFF_HEADS + MOBA_HEADS + DSA_HEADS
D_FF = 4 * D_MODEL
ALPHA = (2.0 * DEPTH) ** 0.25
BETA = (8.0 * DEPTH) ** -0.25
LN_EPS = 1e-5
RMS_EPS = 1e-5
NEG_INF = -1e30
IN_SIZES = (2 * DIFF_HEADS * HEAD_DIM,
            2 * DIFF_HEADS * HEAD_DIM,
            DIFF_HEADS * DIFF_DV,
            MOBA_HEADS * HEAD_DIM,
            MOBA_HEADS * HEAD_DIM,
            MOBA_HEADS * HEAD_DIM,
            DSA_HEADS * HEAD_DIM,
            KV_LATENT,
            IDX_HEADS * IDX_DIM,
            IDX_DIM,
            IDX_HEADS)
D_IN = sum(IN_SIZES)

kernel_name = 'hymba_diff_moba_dsa_deepnorm'


def _layer_norm(x, g, b):
    xf = x.astype(jnp.float32)
    mu = jnp.mean(xf, axis=-1, keepdims=True)
    var = jnp.mean(jnp.square(xf - mu), axis=-1, keepdims=True)
    y = (xf - mu) * lax.rsqrt(var + LN_EPS)
    return (y * g.astype(jnp.float32) + b.astype(jnp.float32)).astype(x.dtype)


def _standardize(x):
    xf = x.astype(jnp.float32)
    mu = jnp.mean(xf, axis=-1, keepdims=True)
    var = jnp.mean(jnp.square(xf - mu), axis=-1, keepdims=True)
    return ((xf - mu) * lax.rsqrt(var + LN_EPS)).astype(x.dtype)


def _rms_norm(x, g):
    xf = x.astype(jnp.float32)
    y = xf * lax.rsqrt(jnp.mean(jnp.square(xf), axis=-1, keepdims=True) + RMS_EPS)
    return (y * g.astype(jnp.float32)).astype(x.dtype)


def _t5_bucket(dist):
    n = jnp.maximum(dist, 0)
    max_exact = NUM_BUCKETS // 2
    nf = jnp.maximum(n, 1).astype(jnp.float32)
    large = max_exact + (jnp.log(nf / max_exact) / math.log(MAX_DISTANCE / max_exact)
                         * (NUM_BUCKETS - max_exact)).astype(jnp.int32)
    large = jnp.minimum(large, NUM_BUCKETS - 1)
    return jnp.where(n < max_exact, n, large)


def diff_attention(q, k, v, lam_vecs, subln_g, tab, lambda_init):
    B, T = q.shape[0], q.shape[1]
    qh = q.transpose(0, 2, 3, 1, 4)
    kh = k.transpose(0, 2, 3, 1, 4)
    vh = v.transpose(0, 2, 1, 3)
    lv = lam_vecs.astype(jnp.float32)
    lam = jnp.exp(jnp.sum(lv[0] * lv[1])) - jnp.exp(jnp.sum(lv[2] * lv[3])) + lambda_init
    key_pos = jnp.arange(T)
    scale = HEAD_DIM ** -0.5

    def block(i):
        q0 = i * DENSE_Q_BLOCK
        qb = lax.dynamic_slice_in_dim(qh, q0, DENSE_Q_BLOCK, axis=3)
        s = jnp.einsum('bhmqd,bhmkd->bhmqk', qb, kh).astype(jnp.float32) * scale
        dist = (q0 + jnp.arange(DENSE_Q_BLOCK))[:, None] - key_pos[None, :]
        bias = tab[_t5_bucket(dist)].reshape(DENSE_Q_BLOCK, T, DIFF_HEADS, 2).transpose(2, 3, 0, 1)
        s = jnp.where(dist >= 0, s + bias.astype(jnp.float32), NEG_INF)
        p = jax.nn.softmax(s, axis=-1)
        a = p[:, :, 0] - lam * p[:, :, 1]
        return jnp.einsum('bhqk,bhkd->bhqd', a.astype(vh.dtype), vh)

    o = lax.map(block, jnp.arange(T // DENSE_Q_BLOCK))
    o = o.transpose(1, 0, 3, 2, 4).reshape(B, T, DIFF_HEADS, DIFF_DV)
    o = _rms_norm(o, subln_g) * (1.0 - lambda_init)
    return o.reshape(B, T, DIFF_HEADS * DIFF_DV)


def moba_attention(q, k, v, tab):
    B, T, H, d = q.shape
    nb = max(-(-T // MOBA_BLOCK), MOBA_TOPK)
    Lp = nb * MOBA_BLOCK
    pad = ((0, 0), (0, 0), (0, Lp - T), (0, 0))
    qh = q.transpose(0, 2, 1, 3)
    kh = jnp.pad(k.transpose(0, 2, 1, 3), pad)
    vh = jnp.pad(v.transpose(0, 2, 1, 3), pad)
    k_blocks = kh.reshape(B, H, nb, MOBA_BLOCK, d)
    v_blocks = vh.reshape(B, H, nb, MOBA_BLOCK, d)
    k_mean = jnp.mean(k_blocks.astype(jnp.float32), axis=3)
    bi = jnp.arange(B)[:, None, None, None]
    hi = jnp.arange(H)[None, :, None, None]
    hi5 = jnp.arange(H)[None, :, None, None, None]
    tab_t = tab.T
    scale = d ** -0.5

    def chunk(i):
        q0 = i * MOBA_Q_CHUNK
        t = q0 + jnp.arange(MOBA_Q_CHUNK)
        own = q0 // MOBA_BLOCK
        qc = lax.dynamic_slice_in_dim(qh, q0, MOBA_Q_CHUNK, axis=2)
        gate = jnp.einsum('bhqd,bhnd->bhqn', qc.astype(jnp.float32), k_mean)
        gate = jnp.where(jnp.arange(nb) < own, gate, NEG_INF)
        _, sel = lax.top_k(gate, MOBA_TOPK)
        valid = jnp.arange(MOBA_TOPK) < own
        ks = k_blocks[bi, hi, sel]
        vs = v_blocks[bi, hi, sel]
        s_sel = jnp.einsum('bhqd,bhqnkd->bhqnk', qc, ks).astype(jnp.float32) * scale
        pos_sel = sel[..., None] * MOBA_BLOCK + jnp.arange(MOBA_BLOCK)
        bias_sel = tab_t[hi5, _t5_bucket(t[None, None, :, None, None] - pos_sel)]
        s_sel = jnp.where(valid[:, None], s_sel + bias_sel.astype(jnp.float32), NEG_INF)
        k_own = lax.dynamic_slice_in_dim(kh, own * MOBA_BLOCK, MOBA_BLOCK, axis=2)
        v_own = lax.dynamic_slice_in_dim(vh, own * MOBA_BLOCK, MOBA_BLOCK, axis=2)
        s_own = jnp.einsum('bhqd,bhkd->bhqk', qc, k_own).astype(jnp.float32) * scale
        dist_own = t[:, None] - (own * MOBA_BLOCK + jnp.arange(MOBA_BLOCK))[None, :]
        bias_own = tab[_t5_bucket(dist_own)].transpose(2, 0, 1)
        s_own = jnp.where(dist_own >= 0, s_own + bias_own.astype(jnp.float32), NEG_INF)
        logits = jnp.concatenate([s_sel.reshape(B, H, MOBA_Q_CHUNK, MOBA_TOPK * MOBA_BLOCK), s_own], axis=-1)
        p = jax.nn.softmax(logits, axis=-1).astype(v.dtype)
        p_sel = p[..., :MOBA_TOPK * MOBA_BLOCK].reshape(B, H, MOBA_Q_CHUNK, MOBA_TOPK, MOBA_BLOCK)
        p_own = p[..., MOBA_TOPK * MOBA_BLOCK:]
        return (jnp.einsum('bhqnk,bhqnkd->bhqd', p_sel, vs)
                + jnp.einsum('bhqk,bhkd->bhqd', p_own, v_own))

    o = lax.map(chunk, jnp.arange(T // MOBA_Q_CHUNK))
    return o.transpose(1, 0, 3, 2, 4).reshape(B, T, H * d)


def dsa_attention(q, c_kv, q_idx, k_idx, w_idx, w_uk, w_uv, tab):
    B, T, H, d = q.shape
    n_top = min(DSA_TOPK, T // 4)
    key_pos = jnp.arange(T)
    bi = jnp.arange(B)[:, None, None]
    scale = d ** -0.5

    def chunk(i):
        q0 = i * DSA_Q_CHUNK
        t = q0 + jnp.arange(DSA_Q_CHUNK)
        qi = lax.dynamic_slice_in_dim(q_idx, q0, DSA_Q_CHUNK, axis=1)
        wi = lax.dynamic_slice_in_dim(w_idx, q0, DSA_Q_CHUNK, axis=1)
        qc = lax.dynamic_slice_in_dim(q, q0, DSA_Q_CHUNK, axis=1)
        rel = jax.nn.relu(jnp.einsum('bqhe,bse->bqhs', qi, k_idx).astype(jnp.float32) * IDX_DIM ** -0.5)
        score = jnp.einsum('bqhs,bqh->bqs', rel, wi.astype(jnp.float32))
        score = jnp.where(key_pos[None, :] <= t[:, None], score, NEG_INF)
        _, sel = lax.top_k(score, n_top)
        valid = jnp.arange(n_top)[None, :] < (t + 1)[:, None]
        c_sel = c_kv[bi, sel]
        q_lat = jnp.einsum('bqhd,hcd->bqhc', qc, w_uk)
        s = jnp.einsum('bqhc,bqkc->bhqk', q_lat, c_sel).astype(jnp.float32) * scale
        bias = tab[_t5_bucket(t[None, :, None] - sel)].transpose(0, 3, 1, 2)
        s = jnp.where(valid, s + bias.astype(jnp.float32), NEG_INF)
        p = jax.nn.softmax(s, axis=-1).astype(c_kv.dtype)
        o_lat = jnp.einsum('bhqk,bqkc->bqhc', p, c_sel)
        return jnp.einsum('bqhc,hcd->bqhd', o_lat, w_uv)

    o = lax.map(chunk, jnp.arange(T // DSA_Q_CHUNK))
    return o.transpose(1, 0, 2, 3, 4).reshape(B, T, H * d)


def hybrid_layer(x, rel_bias, w_in, diff_lambda, diff_subln_g, kv_norm_g, w_uk, w_uv, w_o,
                 ln1_g, ln1_b, w_up, w_down, ln2_g, ln2_b, layer_idx):
    B, T, _ = x.shape
    proj = x @ w_in
    offs = np.cumsum(IN_SIZES)[:-1].tolist()
    dq, dk, dv, mq, mk, mv, cq, ckv, iq, ik, iw = jnp.split(proj, offs, axis=-1)
    lambda_init = 0.8 - 0.6 * math.exp(-0.3 * layer_idx)
    c0 = 2 * DIFF_HEADS
    c1 = c0 + MOBA_HEADS
    y_diff = diff_attention(dq.reshape(B, T, DIFF_HEADS, 2, HEAD_DIM),
                            dk.reshape(B, T, DIFF_HEADS, 2, HEAD_DIM),
                            dv.reshape(B, T, DIFF_HEADS, DIFF_DV),
                            diff_lambda, diff_subln_g, rel_bias[:, :c0], lambda_init)
    y_moba = moba_attention(mq.reshape(B, T, MOBA_HEADS, HEAD_DIM),
                            mk.reshape(B, T, MOBA_HEADS, HEAD_DIM),
                            mv.reshape(B, T, MOBA_HEADS, HEAD_DIM), rel_bias[:, c0:c1])
    y_dsa = dsa_attention(cq.reshape(B, T, DSA_HEADS, HEAD_DIM), _rms_norm(ckv, kv_norm_g),
                          iq.reshape(B, T, IDX_HEADS, IDX_DIM), _standardize(ik),
                          iw * IDX_HEADS ** -0.5, w_uk, w_uv, rel_bias[:, c1:])
    y = jnp.concatenate([y_diff, y_moba, y_dsa], axis=-1) @ w_o
    x = _layer_norm(ALPHA * x + y, ln1_g, ln1_b)
    h = jax.nn.relu(x @ w_up)
    x = _layer_norm(ALPHA * x + (h * h) @ w_down, ln2_g, ln2_b)
    return x


def setup_inputs(seed: int = 0) -> dict:
    key = jax.random.key(seed)
    ks = jax.random.split(key, 17)
    f32 = jnp.float32
    nrm = lambda k, shape: jax.random.normal(k, shape, f32)
    return {
        'x': nrm(ks[0], (BATCH, SEQ, D_MODEL)),
        'ln_emb_g': 1.0 + 0.02 * nrm(ks[1], (D_MODEL,)),
        'ln_emb_b': 0.02 * nrm(ks[2], (D_MODEL,)),
        'rel_bias': 0.2 * nrm(ks[3], (NUM_BUCKETS, N_BIAS_COLS)),
        'w_in': nrm(ks[4], (DEPTH, D_MODEL, D_IN)) * D_MODEL ** -0.5,
        'diff_lambda': 0.1 * nrm(ks[5], (DEPTH, 4, HEAD_DIM)),
        'diff_subln_g': 1.0 + 0.02 * nrm(ks[6], (DEPTH, DIFF_DV)),
        'kv_norm_g': 1.0 + 0.02 * nrm(ks[7], (DEPTH, KV_LATENT)),
        'w_uk': nrm(ks[8], (DEPTH, DSA_HEADS, KV_LATENT, HEAD_DIM)) * KV_LATENT ** -0.5,
        'w_uv': nrm(ks[9], (DEPTH, DSA_HEADS, KV_LATENT, HEAD_DIM)) * KV_LATENT ** -0.5,
        'w_o': nrm(ks[10], (DEPTH, MIX_WIDTH, D_MODEL)) * (MIX_WIDTH ** -0.5 * BETA),
        'ln1_g': 1.0 + 0.02 * nrm(ks[11], (DEPTH, D_MODEL)),
        'ln1_b': 0.02 * nrm(ks[12], (DEPTH, D_MODEL)),
        'w_up': nrm(ks[13], (DEPTH, D_MODEL, D_FF)) * D_MODEL ** -0.5,
        'w_down': nrm(ks[14], (DEPTH, D_FF, D_MODEL)) * (D_FF ** -0.5 * BETA),
        'ln2_g': 1.0 + 0.02 * nrm(ks[15], (DEPTH, D_MODEL)),
        'ln2_b': 0.02 * nrm(ks[16], (DEPTH, D_MODEL)),
    }


def reference(x, ln_emb_g, ln_emb_b, rel_bias, w_in, diff_lambda, diff_subln_g, kv_norm_g,
              w_uk, w_uv, w_o, ln1_g, ln1_b, w_up, w_down, ln2_g, ln2_b):
    h = _layer_norm(x, ln_emb_g, ln_emb_b)
    for l in range(DEPTH):
        h = hybrid_layer(h, rel_bias, w_in[l], diff_lambda[l], diff_subln_g[l], kv_norm_g[l],
                         w_uk[l], w_uv[l], w_o[l], ln1_g[l], ln1_b[l], w_up[l], w_down[l],
                         ln2_g[l], ln2_b[l], l)
    return h
```

```python
import functools
import math

import jax
import jax.numpy as jnp
import numpy as np
from jax import lax
from jax.experimental import pallas as pl
from jax.experimental.pallas import tpu as pltpu

HEAD_DIM = 128
DIFF_HEADS = 6
DIFF_DV = 2 * HEAD_DIM
MOBA_HEADS = 10
DSA_HEADS = 10
MOBA_BLOCK = 256
MOBA_TOPK = 3
DSA_TOPK = 256
KV_LATENT = 512
IDX_HEADS = 32
IDX_DIM = 64
NUM_BUCKETS = 32
MAX_DISTANCE = 128
LN_EPS = 1e-5
RMS_EPS = 1e-5
NEG_INF = -1e30

OFF_DQ = 0
OFF_DK = OFF_DQ + 2 * DIFF_HEADS * HEAD_DIM
OFF_DV = OFF_DK + 2 * DIFF_HEADS * HEAD_DIM
OFF_MQ = OFF_DV + DIFF_HEADS * DIFF_DV
OFF_MK = OFF_MQ + MOBA_HEADS * HEAD_DIM
OFF_MV = OFF_MK + MOBA_HEADS * HEAD_DIM
OFF_CQ = OFF_MV + MOBA_HEADS * HEAD_DIM
OFF_CKV = OFF_CQ + DSA_HEADS * HEAD_DIM
OFF_IQ = OFF_CKV + KV_LATENT
OFF_IK = OFF_IQ + IDX_HEADS * IDX_DIM
OFF_IW = OFF_IK + IDX_DIM
D_IN = OFF_IW + IDX_HEADS
N_MAIN = OFF_IK
TAIL_W = KV_LATENT + 128

LANES = 128
VMEM_LIMIT = 56 * 1024 * 1024
TQ = 256
TK = 256
SEL_TQ = 128
SEL_CH = TK
INT_MIN = -2 ** 31


def _t5_thresholds():
    n = np.arange(0, 4 * MAX_DISTANCE)
    max_exact = NUM_BUCKETS // 2
    nf = np.maximum(n, 1).astype(np.float64)
    large = max_exact + (np.log(nf / max_exact) / math.log(MAX_DISTANCE / max_exact)
                         * (NUM_BUCKETS - max_exact)).astype(np.int32)
    bucket = np.where(n < max_exact, n, np.minimum(large, NUM_BUCKETS - 1))
    return [int(np.argmax(bucket >= b)) for b in range(NUM_BUCKETS)]


T5_THRESHOLDS = _t5_thresholds()
assert T5_THRESHOLDS[-1] <= TK // 2


def _params(*sem):
    return pltpu.CompilerParams(dimension_semantics=sem, vmem_limit_bytes=VMEM_LIMIT)


def _ln_kernel(*refs, scale, has_y):
    if has_y:
        x_ref, y_ref, g_ref, b_ref, o_ref, ob_ref = refs
        z = scale * x_ref[...] + y_ref[...]
    else:
        x_ref, g_ref, b_ref, o_ref, ob_ref = refs
        z = x_ref[...]
    mu = jnp.mean(z, axis=-1, keepdims=True)
    zc = z - mu
    var = jnp.mean(zc * zc, axis=-1, keepdims=True)
    out = zc * lax.rsqrt(var + LN_EPS) * g_ref[...] + b_ref[...]
    o_ref[...] = out
    ob_ref[...] = out.astype(jnp.bfloat16)


def _layer_norm(x, y, g, b, scale, rows=256):
    m, d = x.shape
    row_spec = pl.BlockSpec((rows, d), lambda i: (i, 0))
    vec_spec = pl.BlockSpec((1, d), lambda i: (0, 0))
    has_y = y is not None
    ins = (x, y) if has_y else (x,)
    return pl.pallas_call(
        functools.partial(_ln_kernel, scale=scale, has_y=has_y),
        grid=(m // rows,),
        in_specs=[row_spec] * len(ins) + [vec_spec, vec_spec],
        out_specs=[row_spec, row_spec],
        out_shape=[jax.ShapeDtypeStruct((m, d), jnp.float32), jax.ShapeDtypeStruct((m, d), jnp.bfloat16)],
        compiler_params=_params("parallel"),
        name="layer_norm",
    )(*ins, g.reshape(1, d), b.reshape(1, d))


def _mm_kernel(a_ref, w_ref, o_ref, *, relu2):
    acc = jnp.dot(a_ref[...], w_ref[...], preferred_element_type=jnp.float32)
    if relu2:
        acc = jnp.maximum(acc, 0.0)
        acc = acc * acc
    o_ref[...] = acc.astype(o_ref.dtype)


def _mm_acc_kernel(a_ref, w_ref, o_ref):
    @pl.when(pl.program_id(2) == 0)
    def _():
        o_ref[...] = jnp.zeros_like(o_ref)

    o_ref[...] += jnp.dot(a_ref[...], w_ref[...], preferred_element_type=jnp.float32)


def _matmul(a, w, out_dtype, relu2=False, tm=1024, tn=1024, tk=4096):
    m, k = a.shape
    n = w.shape[1]
    tn = min(tn, n)
    if k == tk:
        return pl.pallas_call(
            functools.partial(_mm_kernel, relu2=relu2),
            grid=(m // tm, n // tn),
            in_specs=[pl.BlockSpec((tm, k), lambda i, j: (i, 0)), pl.BlockSpec((k, tn), lambda i, j: (0, j))],
            out_specs=pl.BlockSpec((tm, tn), lambda i, j: (i, j)),
            out_shape=jax.ShapeDtypeStruct((m, n), out_dtype),
            compiler_params=_params("parallel", "parallel"),
            name="matmul",
        )(a, w)
    assert out_dtype == jnp.float32 and not relu2
    return pl.pallas_call(
        _mm_acc_kernel,
        grid=(m // tm, n // tn, k // tk),
        in_specs=[pl.BlockSpec((tm, tk), lambda i, j, l: (i, l)), pl.BlockSpec((tk, tn), lambda i, j, l: (l, j))],
        out_specs=pl.BlockSpec((tm, tn), lambda i, j, l: (i, j)),
        out_shape=jax.ShapeDtypeStruct((m, n), jnp.float32),
        compiler_params=_params("parallel", "parallel", "arbitrary"),
        name="matmul_acc",
    )(a, w)


def _bias_tiles_kernel(tab_ref, o_ref):
    col = pl.program_id(0)
    row = lax.broadcasted_iota(jnp.int32, (TQ, TK), 0)
    key = lax.broadcasted_iota(jnp.int32, (TQ, TK), 1)
    for tile, shift in ((0, 0), (1, TK)):
        dist = row - key + shift
        bias = jnp.full((TQ, TK), tab_ref[0, col], jnp.float32)
        for b in range(1, NUM_BUCKETS):
            bias = jnp.where(dist >= T5_THRESHOLDS[b], tab_ref[b, col], bias)
        o_ref[0, tile] = jnp.where(dist >= 0, bias, NEG_INF)


def _bias_tiles(rel_bias):
    ncols = rel_bias.shape[1]
    return pl.pallas_call(
        _bias_tiles_kernel,
        grid=(ncols,),
        in_specs=[pl.BlockSpec(memory_space=pltpu.SMEM)],
        out_specs=pl.BlockSpec((1, 2, TQ, TK), lambda c: (c, 0, 0, 0)),
        out_shape=jax.ShapeDtypeStruct((ncols, 2, TQ, TK), jnp.float32),
        compiler_params=_params("parallel"),
        name="t5_bias_tiles",
    )(rel_bias)


def _nt_dot(a, b):
    return lax.dot_general(a, b, (((1,), (1,)), ((), ())), preferred_element_type=jnp.float32)


def _softmax_tile(q, k_tile, v_tile, bias, m_ref, l_ref, acc_ref):
    s = _nt_dot(q, k_tile) * (HEAD_DIM ** -0.5) + bias
    m_old = m_ref[...]
    m_new = jnp.maximum(m_old, jnp.max(s, axis=-1, keepdims=True))
    alpha = jnp.exp(m_old - m_new)
    p = jnp.exp(s - m_new)
    l_ref[...] = alpha * l_ref[...] + jnp.sum(p, axis=-1, keepdims=True)
    acc_ref[...] = alpha * acc_ref[...] + jnp.dot(p.astype(jnp.bfloat16), v_tile,
                                                 preferred_element_type=jnp.float32)
    m_ref[...] = m_new


def _init_state(m_ref, l_ref, acc_ref):
    m_ref[...] = jnp.full_like(m_ref, NEG_INF)
    l_ref[...] = jnp.zeros_like(l_ref)
    acc_ref[...] = jnp.zeros_like(acc_ref)


def _key_rows(ki):
    return pl.ds(pl.multiple_of(ki * TK, TK), TK)


def _diff_kernel(tab_ref, q1_ref, q2_ref, k1_ref, k2_ref, v_ref, bias_ref, lam_ref, g_ref, o_ref,
                 m_ref, l_ref, acc_ref, *, lambda_init):
    h = pl.program_id(1)
    qi = pl.program_id(2)
    q_refs = (q1_ref, q2_ref)
    k_refs = (k1_ref, k2_ref)
    for mp in range(2):
        _init_state(m_ref.at[mp], l_ref.at[mp], acc_ref.at[mp])

    def tile(ki, bias_of_map):
        rows = _key_rows(ki)
        v_tile = v_ref[rows, :]
        for mp in range(2):
            _softmax_tile(q_refs[mp][...], k_refs[mp][rows, :], v_tile, bias_of_map(mp),
                          m_ref.at[mp], l_ref.at[mp], acc_ref.at[mp])

    def far_body(ki, carry):
        tile(ki, lambda mp: tab_ref[NUM_BUCKETS - 1, 2 * h + mp])
        return carry

    lax.fori_loop(0, qi - 1, far_body, 0)

    @pl.when(qi >= 1)
    def _():
        tile(qi - 1, lambda mp: bias_ref[mp, 1])

    tile(qi, lambda mp: bias_ref[mp, 0])

    lv = lam_ref[...]
    lam = (jnp.exp(jnp.sum(lv[0:1] * lv[1:2], axis=-1, keepdims=True))
           - jnp.exp(jnp.sum(lv[2:3] * lv[3:4], axis=-1, keepdims=True)) + lambda_init)
    o = acc_ref[0] / l_ref[0] - lam * (acc_ref[1] / l_ref[1])
    o = o * lax.rsqrt(jnp.mean(o * o, axis=-1, keepdims=True) + RMS_EPS) * g_ref[...]
    o_ref[...] = (o * (1.0 - lambda_init)).astype(o_ref.dtype)


def _diff_attention(proj, rel_bias, bias_tiles, lam_vecs, subln_g, lambda_init, batch, seq):
    nq = seq // TQ
    cb = lambda off: off // HEAD_DIM
    q_spec = lambda mp: pl.BlockSpec((TQ, HEAD_DIM), lambda b, h, i: (b * nq + i, cb(OFF_DQ) + 2 * h + mp))
    k_spec = lambda mp: pl.BlockSpec((seq, HEAD_DIM), lambda b, h, i: (b, cb(OFF_DK) + 2 * h + mp))
    return pl.pallas_call(
        functools.partial(_diff_kernel, lambda_init=lambda_init),
        grid=(batch, DIFF_HEADS, nq),
        in_specs=[
            pl.BlockSpec(memory_space=pltpu.SMEM),
            q_spec(0), q_spec(1), k_spec(0), k_spec(1),
            pl.BlockSpec((seq, DIFF_DV), lambda b, h, i: (b, OFF_DV // DIFF_DV + h)),
            pl.BlockSpec((2, 2, TQ, TK), lambda b, h, i: (h, 0, 0, 0)),
            pl.BlockSpec((4, HEAD_DIM), lambda b, h, i: (0, 0)),
            pl.BlockSpec((1, DIFF_DV), lambda b, h, i: (0, 0)),
        ],
        out_specs=pl.BlockSpec((TQ, DIFF_DV), lambda b, h, i: (b * nq + i, h)),
        out_shape=jax.ShapeDtypeStruct((batch * seq, DIFF_HEADS * DIFF_DV), jnp.bfloat16),
        scratch_shapes=[pltpu.VMEM((2, TQ, 1), jnp.float32), pltpu.VMEM((2, TQ, 1), jnp.float32),
                        pltpu.VMEM((2, TQ, DIFF_DV), jnp.float32)],
        compiler_params=_params("parallel", "parallel", "arbitrary"),
        name="diff_attention",
    )(rel_bias, proj, proj, proj, proj, proj, bias_tiles, lam_vecs, subln_g.reshape(1, DIFF_DV))


def _moba_kernel(tab_ref, q_ref, k_ref, v_ref, bias_ref, o_ref,
                 m_ref, l_ref, acc_ref, kmean_ref, pen_ref, *, col0, nblocks):
    h = pl.program_id(1)
    qi = pl.program_id(2)

    @pl.when(qi == 0)
    def _():
        kmean_ref[...] = jnp.zeros_like(kmean_ref)
        for n in range(nblocks):
            blk = k_ref[n * MOBA_BLOCK:(n + 1) * MOBA_BLOCK, :].astype(jnp.float32)
            kmean_ref[n:n + 1, :] = jnp.mean(blk, axis=0, keepdims=True)

    _init_state(m_ref, l_ref, acc_ref)
    q = q_ref[...]

    kmean = kmean_ref[...]
    kmean_hi = kmean.astype(jnp.bfloat16)
    kmean_lo = (kmean - kmean_hi.astype(jnp.float32)).astype(jnp.bfloat16)
    gate = _nt_dot(q, kmean_hi) + _nt_dot(q, kmean_lo)
    lane = lax.broadcasted_iota(jnp.int32, (TQ, LANES), 1)
    gate = jnp.where(lane < qi, gate, NEG_INF)
    beaten = jnp.zeros((TQ, LANES), jnp.float32)
    for n in range(nblocks):
        other = gate[:, n:n + 1]
        wins = jnp.where(other > gate, 1.0, jnp.where(other == gate, jnp.where(lane > n, 1.0, 0.0), 0.0))
        beaten = beaten + wins
    pen_ref[...] = jnp.where(lane < qi, jnp.where(beaten < MOBA_TOPK, 0.0, NEG_INF), NEG_INF)

    def past_tile(ki, bias):
        pen = jnp.sum(jnp.where(lane == ki, pen_ref[...], 0.0), axis=-1, keepdims=True)
        rows = _key_rows(ki)
        _softmax_tile(q, k_ref[rows, :], v_ref[rows, :], bias + pen, m_ref, l_ref, acc_ref)

    def far_body(ki, carry):
        past_tile(ki, tab_ref[NUM_BUCKETS - 1, col0 + h])
        return carry

    lax.fori_loop(0, qi - 1, far_body, 0)

    @pl.when(qi >= 1)
    def _():
        past_tile(qi - 1, bias_ref[0, 1])

    rows = _key_rows(qi)
    _softmax_tile(q, k_ref[rows, :], v_ref[rows, :], bias_ref[0, 0], m_ref, l_ref, acc_ref)
    o_ref[...] = (acc_ref[...] / l_ref[...]).astype(o_ref.dtype)


def _moba_attention(proj, rel_bias, bias_tiles, col0, batch, seq):
    assert TQ == MOBA_BLOCK and TK == MOBA_BLOCK and seq % MOBA_BLOCK == 0
    nq = seq // TQ
    nblocks = seq // MOBA_BLOCK
    assert MOBA_TOPK <= nblocks <= LANES
    cb = lambda off: off // HEAD_DIM
    kv_spec = lambda off: pl.BlockSpec((seq, HEAD_DIM), lambda b, h, i: (b, cb(off) + h))
    return pl.pallas_call(
        functools.partial(_moba_kernel, col0=col0, nblocks=nblocks),
        grid=(batch, MOBA_HEADS, nq),
        in_specs=[
            pl.BlockSpec(memory_space=pltpu.SMEM),
            pl.BlockSpec((TQ, HEAD_DIM), lambda b, h, i: (b * nq + i, cb(OFF_MQ) + h)),
            kv_spec(OFF_MK), kv_spec(OFF_MV),
            pl.BlockSpec((1, 2, TQ, TK), lambda b, h, i: (col0 + h, 0, 0, 0)),
        ],
        out_specs=pl.BlockSpec((TQ, HEAD_DIM), lambda b, h, i: (b * nq + i, h)),
        out_shape=jax.ShapeDtypeStruct((batch * seq, MOBA_HEADS * HEAD_DIM), jnp.bfloat16),
        scratch_shapes=[pltpu.VMEM((TQ, 1), jnp.float32), pltpu.VMEM((TQ, 1), jnp.float32),
                        pltpu.VMEM((TQ, HEAD_DIM), jnp.float32),
                        pltpu.VMEM((LANES, HEAD_DIM), jnp.float32), pltpu.VMEM((TQ, LANES), jnp.float32)],
        compiler_params=_params("parallel", "parallel", "arbitrary"),
        name="moba_attention",
    )(rel_bias, proj, proj, proj, bias_tiles)


def _dsa_prep_kernel(x_ref, g_ref, wuk_ref, wuv_ref, k_ref, v_ref, iklo_ref, ikhi_ref, iw_ref):
    ckv = x_ref[:, :KV_LATENT]
    ckv = ckv * lax.rsqrt(jnp.mean(ckv * ckv, axis=-1, keepdims=True) + RMS_EPS) * g_ref[...]
    ckv = ckv.astype(jnp.bfloat16)
    k_ref[...] = jnp.dot(ckv, wuk_ref[...], preferred_element_type=jnp.float32).astype(k_ref.dtype)
    v_ref[...] = jnp.dot(ckv, wuv_ref[...], preferred_element_type=jnp.float32).astype(v_ref.dtype)

    grp = x_ref[:, KV_LATENT:]
    lane = lax.broadcasted_iota(jnp.int32, grp.shape, 1)
    is_key = lane < IDX_DIM
    mu = jnp.sum(jnp.where(is_key, grp, 0.0), axis=-1, keepdims=True) / IDX_DIM
    cen = jnp.where(is_key, grp - mu, 0.0)
    var = jnp.sum(cen * cen, axis=-1, keepdims=True) / IDX_DIM
    key_lo = cen * lax.rsqrt(var + LN_EPS)
    iklo_ref[...] = key_lo.astype(iklo_ref.dtype)
    ikhi_ref[...] = pltpu.roll(key_lo, IDX_DIM, axis=1).astype(ikhi_ref.dtype)
    w = pltpu.roll(grp, LANES - IDX_DIM, axis=1)
    iw_ref[...] = jnp.where(lane < IDX_HEADS, w * (IDX_HEADS ** -0.5) * (IDX_DIM ** -0.5), 0.0)


def _dsa_prep(tail, kv_norm_g, wuk_flat, wuv_flat, rows=512):
    m = tail.shape[0]
    n = DSA_HEADS * HEAD_DIM
    row = lambda w: pl.BlockSpec((rows, w), lambda i: (i, 0))
    full = lambda a: pl.BlockSpec(a.shape, lambda i: (0, 0))
    g = kv_norm_g.reshape(1, KV_LATENT)
    return pl.pallas_call(
        _dsa_prep_kernel,
        grid=(m // rows,),
        in_specs=[row(TAIL_W), full(g), full(wuk_flat), full(wuv_flat)],
        out_specs=[row(n), row(n), row(LANES), row(LANES), row(LANES)],
        out_shape=[jax.ShapeDtypeStruct((m, n), jnp.bfloat16), jax.ShapeDtypeStruct((m, n), jnp.bfloat16),
                   jax.ShapeDtypeStruct((m, LANES), jnp.bfloat16), jax.ShapeDtypeStruct((m, LANES), jnp.bfloat16),
                   jax.ShapeDtypeStruct((m, LANES), jnp.float32)],
        compiler_params=_params("parallel"),
        name="dsa_prep",
    )(tail, g, wuk_flat, wuv_flat)


def _sortable(x):
    b = pltpu.bitcast(x, jnp.int32)
    return b ^ ((b >> 31) & jnp.int32(0x7FFFFFFF))


def _dsa_select_kernel(iq_ref, iklo_ref, ikhi_ref, iw_ref, o_ref, key_ref, eqidx_ref, *, n_top):
    qi = pl.program_id(1)
    n_ch = (qi * SEL_TQ + SEL_TQ - 1) // SEL_CH + 1
    t = qi * SEL_TQ + lax.broadcasted_iota(jnp.int32, (SEL_TQ, SEL_CH), 0)
    lane = lax.broadcasted_iota(jnp.int32, (SEL_TQ, SEL_CH), 1)
    chunk = lambda c: pl.ds(pl.multiple_of(c * SEL_CH, SEL_CH), SEL_CH)

    def score_body(c, carry):
        rows = chunk(c)
        k_lo = iklo_ref[rows, :]
        k_hi = ikhi_ref[rows, :]
        acc = jnp.zeros((SEL_TQ, SEL_CH), jnp.float32)
        for pair in range(IDX_HEADS // 2):
            q2 = iq_ref[:, pair * LANES:(pair + 1) * LANES]
            for half, k_half in enumerate((k_lo, k_hi)):
                hd = 2 * pair + half
                acc = acc + jnp.maximum(_nt_dot(q2, k_half), 0.0) * iw_ref[:, hd:hd + 1]
        score = jnp.where(c * SEL_CH + lane <= t, acc, NEG_INF)
        key_ref[c] = _sortable(score)
        return carry

    lax.fori_loop(0, n_ch, score_body, 0)

    def count(ref, pred):
        def body(c, acc):
            hit = jnp.where(pred(ref[c]), 1.0, 0.0)
            for j in range(SEL_CH // LANES):
                acc = acc + hit[:, j * LANES:(j + 1) * LANES]
            return acc
        acc = lax.fori_loop(0, n_ch, body, jnp.zeros((SEL_TQ, LANES), jnp.float32))
        return jnp.sum(acc, axis=-1, keepdims=True)

    v = jnp.where(count(key_ref, lambda x: x >= 0) >= n_top, jnp.int32(0), jnp.int32(INT_MIN))

    def bit_body(i, v):
        trial = v + jnp.left_shift(jnp.int32(1), 30 - i)
        return jnp.where(count(key_ref, lambda x: x >= trial) >= n_top, trial, v)

    v = lax.fori_loop(0, 31, bit_body, v)

    need = n_top - count(key_ref, lambda x: x > v)
    far = jnp.int32(2 ** 30)

    def eq_body(c, carry):
        eqidx_ref[c] = jnp.where(key_ref[c] == v, c * SEL_CH + lane, far)
        return carry

    lax.fori_loop(0, n_ch, eq_body, 0)
    nbits = max(1, (key_ref.shape[0] * SEL_CH - 1).bit_length())

    def idx_body(i, cut):
        trial = cut + jnp.left_shift(jnp.int32(1), nbits - 1 - i)
        return jnp.where(count(eqidx_ref, lambda x: x < trial) <= need - 1.0, trial, cut)

    cut = lax.fori_loop(0, nbits, idx_body, jnp.zeros((SEL_TQ, 1), jnp.int32))

    o_ref[...] = jnp.full_like(o_ref, NEG_INF)

    def out_body(c, carry):
        picked = jnp.where(key_ref[c] > v, 0.0, jnp.where(eqidx_ref[c] <= cut, 0.0, NEG_INF))
        o_ref[0, c] = jnp.where(c * SEL_CH + lane <= t, picked, NEG_INF)
        return carry

    lax.fori_loop(0, n_ch, out_body, 0)


def _dsa_select(proj, ik_lo, ik_hi, iw, batch, seq):
    assert seq % SEL_CH == 0 and seq >= SEL_CH >= DSA_TOPK
    n_top = min(DSA_TOPK, seq // 4)
    nq = seq // SEL_TQ
    n_ch = seq // SEL_CH
    iq_w = IDX_HEADS * IDX_DIM
    key_spec = pl.BlockSpec((seq, LANES), lambda b, i: (b, 0))
    return pl.pallas_call(
        functools.partial(_dsa_select_kernel, n_top=n_top),
        grid=(batch, nq),
        in_specs=[pl.BlockSpec((SEL_TQ, iq_w), lambda b, i: (b * nq + i, OFF_IQ // iq_w)),
                  key_spec, key_spec,
                  pl.BlockSpec((SEL_TQ, LANES), lambda b, i: (b * nq + i, 0))],
        out_specs=pl.BlockSpec((1, n_ch, SEL_TQ, SEL_CH), lambda b, i: (b, 0, i, 0)),
        out_shape=jax.ShapeDtypeStruct((batch, n_ch, seq, SEL_CH), jnp.float32),
        scratch_shapes=[pltpu.VMEM((n_ch, SEL_TQ, SEL_CH), jnp.int32), pltpu.VMEM((n_ch, SEL_TQ, SEL_CH), jnp.int32)],
        compiler_params=_params("parallel", "arbitrary"),
        name="dsa_select",
    )(proj, ik_lo, ik_hi, iw)


def _dsa_attn_kernel(tab_ref, q_ref, k_ref, v_ref, bias_ref, mask_ref, o_ref, m_ref, l_ref, acc_ref, *, col0):
    h = pl.program_id(1)
    qi = pl.program_id(2)
    _init_state(m_ref, l_ref, acc_ref)
    q = q_ref[...]

    def tile(ki, bias):
        rows = _key_rows(ki)
        _softmax_tile(q, k_ref[rows, :], v_ref[rows, :], bias + mask_ref[0, ki], m_ref, l_ref, acc_ref)

    def far_body(ki, carry):
        tile(ki, tab_ref[NUM_BUCKETS - 1, col0 + h])
        return carry

    lax.fori_loop(0, qi - 1, far_body, 0)

    @pl.when(qi >= 1)
    def _():
        tile(qi - 1, bias_ref[0, 1])

    tile(qi, bias_ref[0, 0])
    o_ref[...] = (acc_ref[...] / l_ref[...]).astype(o_ref.dtype)


def _dsa_attention(proj, k_dsa, v_dsa, mask, rel_bias, bias_tiles, col0, batch, seq):
    nq = seq // TQ
    kv_spec = pl.BlockSpec((seq, HEAD_DIM), lambda b, h, i: (b, h))
    return pl.pallas_call(
        functools.partial(_dsa_attn_kernel, col0=col0),
        grid=(batch, DSA_HEADS, nq),
        in_specs=[
            pl.BlockSpec(memory_space=pltpu.SMEM),
            pl.BlockSpec((TQ, HEAD_DIM), lambda b, h, i: (b * nq + i, OFF_CQ // HEAD_DIM + h)),
            kv_spec, kv_spec,
            pl.BlockSpec((1, 2, TQ, TK), lambda b, h, i: (col0 + h, 0, 0, 0)),
            pl.BlockSpec((1, seq // TK, TQ, TK), lambda b, h, i: (b, 0, i, 0)),
        ],
        out_specs=pl.BlockSpec((TQ, HEAD_DIM), lambda b, h, i: (b * nq + i, h)),
        out_shape=jax.ShapeDtypeStruct((batch * seq, DSA_HEADS * HEAD_DIM), jnp.bfloat16),
        scratch_shapes=[pltpu.VMEM((TQ, 1), jnp.float32), pltpu.VMEM((TQ, 1), jnp.float32),
                        pltpu.VMEM((TQ, HEAD_DIM), jnp.float32)],
        compiler_params=_params("parallel", "parallel", "arbitrary"),
        name="dsa_attention",
    )(rel_bias, proj, k_dsa, v_dsa, bias_tiles, mask)


def _mixers(proj, tail, rel_bias, bias_tiles, diff_lambda, diff_subln_g, kv_norm_g, w_uk, w_uv,
            layer_idx, batch, seq):
    lambda_init = 0.8 - 0.6 * math.exp(-0.3 * layer_idx)
    c0 = 2 * DIFF_HEADS
    c1 = c0 + MOBA_HEADS
    flat = lambda w: w.transpose(1, 0, 2).reshape(KV_LATENT, DSA_HEADS * HEAD_DIM).astype(jnp.bfloat16)
    y_diff = _diff_attention(proj, rel_bias, bias_tiles, diff_lambda, diff_subln_g, lambda_init, batch, seq)
    y_moba = _moba_attention(proj, rel_bias, bias_tiles, c0, batch, seq)
    k_dsa, v_dsa, ik_lo, ik_hi, iw = _dsa_prep(tail, kv_norm_g, flat(w_uk), flat(w_uv))
    mask = _dsa_select(proj, ik_lo, ik_hi, iw, batch, seq)
    y_dsa = _dsa_attention(proj, k_dsa, v_dsa, mask, rel_bias, bias_tiles, c1, batch, seq)
    return jnp.concatenate([y_diff, y_moba, y_dsa], axis=-1)


def _split_w_in(w_in):
    w_main = w_in[:, :N_MAIN].astype(jnp.bfloat16)
    pad = jnp.zeros((w_in.shape[0], TAIL_W - (D_IN - OFF_IK) - KV_LATENT), w_in.dtype)
    w_tail = jnp.concatenate([w_in[:, OFF_CKV:OFF_IQ], w_in[:, OFF_IK:], pad], axis=1).astype(jnp.bfloat16)
    return w_main, w_tail


def kernel(x, ln_emb_g, ln_emb_b, rel_bias, w_in, diff_lambda, diff_subln_g, kv_norm_g, w_uk, w_uv, w_o,
           ln1_g, ln1_b, w_up, w_down, ln2_g, ln2_b):
    batch, seq, d_model = x.shape
    depth = w_in.shape[0]
    alpha = (2.0 * depth) ** 0.25
    bf16 = jnp.bfloat16
    bias_tiles = _bias_tiles(rel_bias)
    h, hb = _layer_norm(x.reshape(batch * seq, d_model), None, ln_emb_g, ln_emb_b, 1.0)
    for l in range(depth):
        w_main, w_tail = _split_w_in(w_in[l])
        proj = _matmul(hb, w_main, bf16)
        tail = _matmul(hb, w_tail, jnp.float32)
        mix = _mixers(proj, tail, rel_bias, bias_tiles, diff_lambda[l], diff_subln_g[l], kv_norm_g[l],
                      w_uk[l], w_uv[l], l, batch, seq)
        y = _matmul(mix, w_o[l].astype(bf16), jnp.float32)
        h, hb = _layer_norm(h, y, ln1_g[l], ln1_b[l], alpha)
        up = _matmul(hb, w_up[l].astype(bf16), bf16, relu2=True)
        y = _matmul(up, w_down[l].astype(bf16), jnp.float32)
        h, hb = _layer_norm(h, y, ln2_g[l], ln2_b[l], alpha)
    return h.reshape(batch, seq, d_model)
```

```python
import functools
import math

import jax
import jax.numpy as jnp
import numpy as np
from jax import lax
from jax.experimental import pallas as pl
from jax.experimental.pallas import tpu as pltpu

HEAD_DIM = 128
DIFF_HEADS = 6
DIFF_DV = 2 * HEAD_DIM
MOBA_HEADS = 10
DSA_HEADS = 10
MOBA_BLOCK = 256
MOBA_TOPK = 3
DSA_TOPK = 256
KV_LATENT = 512
IDX_HEADS = 32
IDX_DIM = 64
NUM_BUCKETS = 32
MAX_DISTANCE = 128
LN_EPS = 1e-5
RMS_EPS = 1e-5
NEG_INF = -1e30

OFF_DQ = 0
OFF_DK = OFF_DQ + 2 * DIFF_HEADS * HEAD_DIM
OFF_DV = OFF_DK + 2 * DIFF_HEADS * HEAD_DIM
OFF_MQ = OFF_DV + DIFF_HEADS * DIFF_DV
OFF_MK = OFF_MQ + MOBA_HEADS * HEAD_DIM
OFF_MV = OFF_MK + MOBA_HEADS * HEAD_DIM
OFF_CQ = OFF_MV + MOBA_HEADS * HEAD_DIM
OFF_CKV = OFF_CQ + DSA_HEADS * HEAD_DIM
OFF_IQ = OFF_CKV + KV_LATENT
OFF_IK = OFF_IQ + IDX_HEADS * IDX_DIM
OFF_IW = OFF_IK + IDX_DIM
D_IN = OFF_IW + IDX_HEADS
N_MAIN = OFF_IK
TAIL_W = KV_LATENT + 128

LANES = 128
VMEM_LIMIT = 56 * 1024 * 1024
TQ = 256
TK = 256
SEL_TQ = 128
SEL_CH = TK
INT_MIN = -2 ** 31
FAR_TILES = 4
EXP2_SCALE = HEAD_DIM ** -0.5 * math.log2(math.e)


def _t5_thresholds():
    n = np.arange(0, 4 * MAX_DISTANCE)
    max_exact = NUM_BUCKETS // 2
    nf = np.maximum(n, 1).astype(np.float64)
    large = max_exact + (np.log(nf / max_exact) / math.log(MAX_DISTANCE / max_exact)
                         * (NUM_BUCKETS - max_exact)).astype(np.int32)
    bucket = np.where(n < max_exact, n, np.minimum(large, NUM_BUCKETS - 1))
    return [int(np.argmax(bucket >= b)) for b in range(NUM_BUCKETS)]


T5_THRESHOLDS = _t5_thresholds()
assert T5_THRESHOLDS[-1] <= TK // 2


def _params(*sem):
    return pltpu.CompilerParams(dimension_semantics=sem, vmem_limit_bytes=VMEM_LIMIT)


def _ln_kernel(*refs, scale, has_y):
    if has_y:
        x_ref, y_ref, g_ref, b_ref, o_ref, ob_ref = refs
        z = scale * x_ref[...] + y_ref[...]
    else:
        x_ref, g_ref, b_ref, o_ref, ob_ref = refs
        z = x_ref[...]
    mu = jnp.mean(z, axis=-1, keepdims=True)
    zc = z - mu
    var = jnp.mean(zc * zc, axis=-1, keepdims=True)
    out = zc * lax.rsqrt(var + LN_EPS) * g_ref[...] + b_ref[...]
    o_ref[...] = out
    ob_ref[...] = out.astype(jnp.bfloat16)


def _layer_norm(x, y, g, b, scale, rows=256):
    m, d = x.shape
    row_spec = pl.BlockSpec((rows, d), lambda i: (i, 0))
    vec_spec = pl.BlockSpec((1, d), lambda i: (0, 0))
    has_y = y is not None
    ins = (x, y) if has_y else (x,)
    return pl.pallas_call(
        functools.partial(_ln_kernel, scale=scale, has_y=has_y),
        grid=(m // rows,),
        in_specs=[row_spec] * len(ins) + [vec_spec, vec_spec],
        out_specs=[row_spec, row_spec],
        out_shape=[jax.ShapeDtypeStruct((m, d), jnp.float32), jax.ShapeDtypeStruct((m, d), jnp.bfloat16)],
        compiler_params=_params("parallel"),
        name="layer_norm",
    )(*ins, g.reshape(1, d), b.reshape(1, d))


def _mm_kernel(a_ref, w_ref, o_ref, *, relu2):
    acc = jnp.dot(a_ref[...], w_ref[...], preferred_element_type=jnp.float32)
    if relu2:
        acc = jnp.maximum(acc, 0.0)
        acc = acc * acc
    o_ref[...] = acc.astype(o_ref.dtype)


def _mm_acc_kernel(a_ref, w_ref, o_ref):
    @pl.when(pl.program_id(2) == 0)
    def _():
        o_ref[...] = jnp.zeros_like(o_ref)

    o_ref[...] += jnp.dot(a_ref[...], w_ref[...], preferred_element_type=jnp.float32)


def _matmul(a, w, out_dtype, relu2=False, tm=1024, tn=1024, tk=4096):
    m, k = a.shape
    n = w.shape[1]
    tn = min(tn, n)
    if k == tk:
        return pl.pallas_call(
            functools.partial(_mm_kernel, relu2=relu2),
            grid=(m // tm, n // tn),
            in_specs=[pl.BlockSpec((tm, k), lambda i, j: (i, 0)), pl.BlockSpec((k, tn), lambda i, j: (0, j))],
            out_specs=pl.BlockSpec((tm, tn), lambda i, j: (i, j)),
            out_shape=jax.ShapeDtypeStruct((m, n), out_dtype),
            compiler_params=_params("parallel", "parallel"),
            name="matmul",
        )(a, w)
    assert out_dtype == jnp.float32 and not relu2
    return pl.pallas_call(
        _mm_acc_kernel,
        grid=(m // tm, n // tn, k // tk),
        in_specs=[pl.BlockSpec((tm, tk), lambda i, j, l: (i, l)), pl.BlockSpec((tk, tn), lambda i, j, l: (l, j))],
        out_specs=pl.BlockSpec((tm, tn), lambda i, j, l: (i, j)),
        out_shape=jax.ShapeDtypeStruct((m, n), jnp.float32),
        compiler_params=_params("parallel", "parallel", "arbitrary"),
        name="matmul_acc",
    )(a, w)


def _bias_tiles_kernel(tab_ref, o_ref):
    col = pl.program_id(0)
    row = lax.broadcasted_iota(jnp.int32, (TQ, TK), 0)
    key = lax.broadcasted_iota(jnp.int32, (TQ, TK), 1)
    last = tab_ref[NUM_BUCKETS - 1, col]
    for tile, shift in ((0, 0), (1, TK)):
        dist = row - key + shift
        bias = jnp.full((TQ, TK), tab_ref[0, col], jnp.float32)
        for b in range(1, NUM_BUCKETS):
            bias = jnp.where(dist >= T5_THRESHOLDS[b], tab_ref[b, col], bias)
        o_ref[0, tile] = jnp.where(dist >= 0, (bias - last) * (HEAD_DIM ** 0.5), NEG_INF)


def _bias_tiles(rel_bias):
    ncols = rel_bias.shape[1]
    return pl.pallas_call(
        _bias_tiles_kernel,
        grid=(ncols,),
        in_specs=[pl.BlockSpec(memory_space=pltpu.SMEM)],
        out_specs=pl.BlockSpec((1, 2, TQ, TK), lambda c: (c, 0, 0, 0)),
        out_shape=jax.ShapeDtypeStruct((ncols, 2, TQ, TK), jnp.float32),
        compiler_params=_params("parallel"),
        name="t5_bias_tiles",
    )(rel_bias)


def _nt_dot(a, b):
    return lax.dot_general(a, b, (((1,), (1,)), ((), ())), preferred_element_type=jnp.float32)


def _lanes(x, n):
    return x if n == LANES else jnp.concatenate([x] * (n // LANES), axis=1)


def _with_ones(v):
    return jnp.concatenate([v, jnp.ones((v.shape[0], LANES), v.dtype)], axis=1)


def _softmax_tile(z, v_ones, m_ref, acc_ref):
    m_old = m_ref[...]
    m_new = jnp.maximum(m_old, jnp.max(z, axis=-1, keepdims=True))
    alpha = jnp.exp2((m_old - m_new) * EXP2_SCALE)
    p = jnp.exp2((z - _lanes(m_new, z.shape[1])) * EXP2_SCALE)
    acc_ref[...] = _lanes(alpha, acc_ref.shape[1]) * acc_ref[...] + jnp.dot(
        p.astype(jnp.bfloat16), v_ones, preferred_element_type=jnp.float32)
    m_ref[...] = m_new


def _init_state(m_ref, acc_ref):
    m_ref[...] = jnp.full_like(m_ref, NEG_INF)
    acc_ref[...] = jnp.zeros_like(acc_ref)


def _normalized(acc_ref, dv):
    acc = acc_ref[...]
    return acc[:, :dv] / _lanes(acc[:, dv:], dv)


def _key_rows(k0, ntiles):
    return pl.ds(pl.multiple_of(k0 * TK, TK), ntiles * TK)


def _causal_key_loop(qi, tile_fn):
    n_far = jnp.maximum(qi - 1, 0)
    n_big = n_far // FAR_TILES

    def big_body(kb, carry):
        tile_fn(kb * FAR_TILES, FAR_TILES, None)
        return carry

    def small_body(ki, carry):
        tile_fn(ki, 1, None)
        return carry

    lax.fori_loop(0, n_big, big_body, 0)
    lax.fori_loop(n_big * FAR_TILES, n_far, small_body, 0)

    @pl.when(qi >= 1)
    def _():
        tile_fn(qi - 1, 1, 1)

    tile_fn(qi, 1, 0)


def _diff_kernel(q1_ref, q2_ref, k1_ref, k2_ref, v_ref, bias_ref, lam_ref, g_ref, o_ref, m_ref, acc_ref,
                 *, lambda_init):
    qi = pl.program_id(2)
    q_refs = (q1_ref, q2_ref)
    k_refs = (k1_ref, k2_ref)
    for mp in range(2):
        _init_state(m_ref.at[mp], acc_ref.at[mp])

    def tile(k0, ntiles, bias_slot):
        rows = _key_rows(k0, ntiles)
        v_ones = _with_ones(v_ref[rows, :])
        for mp in range(2):
            z = _nt_dot(q_refs[mp][...], k_refs[mp][rows, :])
            if bias_slot is not None:
                z = z + bias_ref[mp, bias_slot]
            _softmax_tile(z, v_ones, m_ref.at[mp], acc_ref.at[mp])

    _causal_key_loop(qi, tile)

    lv = lam_ref[...]
    lam = (jnp.exp(jnp.sum(lv[0:1] * lv[1:2], axis=-1, keepdims=True))
           - jnp.exp(jnp.sum(lv[2:3] * lv[3:4], axis=-1, keepdims=True)) + lambda_init)
    o = _normalized(acc_ref.at[0], DIFF_DV) - lam * _normalized(acc_ref.at[1], DIFF_DV)
    o = o * lax.rsqrt(jnp.mean(o * o, axis=-1, keepdims=True) + RMS_EPS) * g_ref[...]
    o_ref[...] = (o * (1.0 - lambda_init)).astype(o_ref.dtype)


def _diff_attention(proj, bias_tiles, lam_vecs, subln_g, lambda_init, batch, seq):
    nq = seq // TQ
    cb = lambda off: off // HEAD_DIM
    q_spec = lambda mp: pl.BlockSpec((TQ, HEAD_DIM), lambda b, h, i: (b * nq + i, cb(OFF_DQ) + 2 * h + mp))
    k_spec = lambda mp: pl.BlockSpec((seq, HEAD_DIM), lambda b, h, i: (b, cb(OFF_DK) + 2 * h + mp))
    return pl.pallas_call(
        functools.partial(_diff_kernel, lambda_init=lambda_init),
        grid=(batch, DIFF_HEADS, nq),
        in_specs=[
            q_spec(0), q_spec(1), k_spec(0), k_spec(1),
            pl.BlockSpec((seq, DIFF_DV), lambda b, h, i: (b, OFF_DV // DIFF_DV + h)),
            pl.BlockSpec((2, 2, TQ, TK), lambda b, h, i: (h, 0, 0, 0)),
            pl.BlockSpec((4, HEAD_DIM), lambda b, h, i: (0, 0)),
            pl.BlockSpec((1, DIFF_DV), lambda b, h, i: (0, 0)),
        ],
        out_specs=pl.BlockSpec((TQ, DIFF_DV), lambda b, h, i: (b * nq + i, h)),
        out_shape=jax.ShapeDtypeStruct((batch * seq, DIFF_HEADS * DIFF_DV), jnp.bfloat16),
        scratch_shapes=[pltpu.VMEM((2, TQ, LANES), jnp.float32),
                        pltpu.VMEM((2, TQ, DIFF_DV + LANES), jnp.float32)],
        compiler_params=_params("parallel", "parallel", "arbitrary"),
        name="diff_attention",
    )(proj, proj, proj, proj, proj, bias_tiles, lam_vecs, subln_g.reshape(1, DIFF_DV))


def _moba_kernel(q_ref, k_ref, v_ref, bias_ref, o_ref, m_ref, acc_ref, kmean_ref, *, nblocks):
    qi = pl.program_id(2)

    @pl.when(qi == 0)
    def _():
        kmean_ref[...] = jnp.zeros_like(kmean_ref)
        for n in range(nblocks):
            blk = k_ref[n * MOBA_BLOCK:(n + 1) * MOBA_BLOCK, :].astype(jnp.float32)
            kmean_ref[n:n + 1, :] = jnp.mean(blk, axis=0, keepdims=True)

    _init_state(m_ref, acc_ref)
    q = q_ref[...]

    kmean = kmean_ref[...]
    kmean_hi = kmean.astype(jnp.bfloat16)
    kmean_lo = (kmean - kmean_hi.astype(jnp.float32)).astype(jnp.bfloat16)
    gate = _nt_dot(q, kmean_hi) + _nt_dot(q, kmean_lo)
    lane = lax.broadcasted_iota(jnp.int32, (TQ, LANES), 1)
    gate = jnp.where(lane < qi, gate, NEG_INF)
    beaten = jnp.zeros((TQ, LANES), jnp.float32)
    for n in range(nblocks):
        other = gate[:, n:n + 1]
        wins = jnp.where(other > gate, 1.0, jnp.where(other == gate, jnp.where(lane > n, 1.0, 0.0), 0.0))
        beaten = beaten + wins
    past_pen = jnp.where(beaten < MOBA_TOPK, 0.0, NEG_INF)
    own_pen = jnp.where(lane == qi, 0.0, NEG_INF)
    q_aug = jnp.concatenate([q, jnp.where(lane < qi, past_pen, own_pen).astype(jnp.bfloat16)], axis=1)

    def tile(k0, ntiles, bias_slot):
        rows = _key_rows(k0, ntiles)
        n = ntiles * TK
        block_of_key = k0 + lax.broadcasted_iota(jnp.int32, (n, LANES), 0) // MOBA_BLOCK
        one_hot = jnp.where(lax.broadcasted_iota(jnp.int32, (n, LANES), 1) == block_of_key, 1.0, 0.0)
        z = _nt_dot(q_aug, jnp.concatenate([k_ref[rows, :], one_hot.astype(jnp.bfloat16)], axis=1))
        if bias_slot is not None:
            z = z + bias_ref[0, bias_slot]
        _softmax_tile(z, _with_ones(v_ref[rows, :]), m_ref, acc_ref)

    _causal_key_loop(qi, tile)
    o_ref[...] = _normalized(acc_ref, HEAD_DIM).astype(o_ref.dtype)


def _moba_attention(proj, bias_tiles, col0, batch, seq):
    assert TQ == MOBA_BLOCK and TK == MOBA_BLOCK and seq % MOBA_BLOCK == 0
    nq = seq // TQ
    nblocks = seq // MOBA_BLOCK
    assert MOBA_TOPK <= nblocks <= LANES
    cb = lambda off: off // HEAD_DIM
    kv_spec = lambda off: pl.BlockSpec((seq, HEAD_DIM), lambda b, h, i: (b, cb(off) + h))
    return pl.pallas_call(
        functools.partial(_moba_kernel, nblocks=nblocks),
        grid=(batch, MOBA_HEADS, nq),
        in_specs=[
            pl.BlockSpec((TQ, HEAD_DIM), lambda b, h, i: (b * nq + i, cb(OFF_MQ) + h)),
            kv_spec(OFF_MK), kv_spec(OFF_MV),
            pl.BlockSpec((1, 2, TQ, TK), lambda b, h, i: (col0 + h, 0, 0, 0)),
        ],
        out_specs=pl.BlockSpec((TQ, HEAD_DIM), lambda b, h, i: (b * nq + i, h)),
        out_shape=jax.ShapeDtypeStruct((batch * seq, MOBA_HEADS * HEAD_DIM), jnp.bfloat16),
        scratch_shapes=[pltpu.VMEM((TQ, LANES), jnp.float32), pltpu.VMEM((TQ, HEAD_DIM + LANES), jnp.float32),
                        pltpu.VMEM((LANES, HEAD_DIM), jnp.float32)],
        compiler_params=_params("parallel", "parallel", "arbitrary"),
        name="moba_attention",
    )(proj, proj, proj, bias_tiles)


def _dsa_prep_kernel(x_ref, g_ref, wuk_ref, wuv_ref, k_ref, v_ref, iklo_ref, ikhi_ref, iw_ref):
    ckv = x_ref[:, :KV_LATENT]
    ckv = ckv * lax.rsqrt(jnp.mean(ckv * ckv, axis=-1, keepdims=True) + RMS_EPS) * g_ref[...]
    ckv = ckv.astype(jnp.bfloat16)
    k_ref[...] = jnp.dot(ckv, wuk_ref[...], preferred_element_type=jnp.float32).astype(k_ref.dtype)
    v_ref[...] = jnp.dot(ckv, wuv_ref[...], preferred_element_type=jnp.float32).astype(v_ref.dtype)

    grp = x_ref[:, KV_LATENT:]
    lane = lax.broadcasted_iota(jnp.int32, grp.shape, 1)
    is_key = lane < IDX_DIM
    mu = jnp.sum(jnp.where(is_key, grp, 0.0), axis=-1, keepdims=True) / IDX_DIM
    cen = jnp.where(is_key, grp - mu, 0.0)
    var = jnp.sum(cen * cen, axis=-1, keepdims=True) / IDX_DIM
    key_lo = cen * lax.rsqrt(var + LN_EPS)
    iklo_ref[...] = key_lo.astype(iklo_ref.dtype)
    ikhi_ref[...] = pltpu.roll(key_lo, IDX_DIM, axis=1).astype(ikhi_ref.dtype)
    w = pltpu.roll(grp, LANES - IDX_DIM, axis=1)
    iw_ref[...] = jnp.where(lane < IDX_HEADS, w * (IDX_HEADS ** -0.5) * (IDX_DIM ** -0.5), 0.0)


def _dsa_prep(tail, kv_norm_g, wuk_flat, wuv_flat, rows=512):
    m = tail.shape[0]
    n = DSA_HEADS * HEAD_DIM
    row = lambda w: pl.BlockSpec((rows, w), lambda i: (i, 0))
    full = lambda a: pl.BlockSpec(a.shape, lambda i: (0, 0))
    g = kv_norm_g.reshape(1, KV_LATENT)
    return pl.pallas_call(
        _dsa_prep_kernel,
        grid=(m // rows,),
        in_specs=[row(TAIL_W), full(g), full(wuk_flat), full(wuv_flat)],
        out_specs=[row(n), row(n), row(LANES), row(LANES), row(LANES)],
        out_shape=[jax.ShapeDtypeStruct((m, n), jnp.bfloat16), jax.ShapeDtypeStruct((m, n), jnp.bfloat16),
                   jax.ShapeDtypeStruct((m, LANES), jnp.bfloat16), jax.ShapeDtypeStruct((m, LANES), jnp.bfloat16),
                   jax.ShapeDtypeStruct((m, LANES), jnp.float32)],
        compiler_params=_params("parallel"),
        name="dsa_prep",
    )(tail, g, wuk_flat, wuv_flat)


def _sortable(x):
    b = pltpu.bitcast(x, jnp.int32)
    return b ^ ((b >> 31) & jnp.int32(0x7FFFFFFF))


def _dsa_select_kernel(iq_ref, iklo_ref, ikhi_ref, iw_ref, o_ref, key_ref, eqidx_ref, *, n_top):
    qi = pl.program_id(1)
    n_ch = (qi * SEL_TQ + SEL_TQ - 1) // SEL_CH + 1
    t = qi * SEL_TQ + lax.broadcasted_iota(jnp.int32, (SEL_TQ, SEL_CH), 0)
    lane = lax.broadcasted_iota(jnp.int32, (SEL_TQ, SEL_CH), 1)
    chunk = lambda c: pl.ds(pl.multiple_of(c * SEL_CH, SEL_CH), SEL_CH)

    def score_body(c, carry):
        rows = chunk(c)
        k_lo = iklo_ref[rows, :]
        k_hi = ikhi_ref[rows, :]
        acc = jnp.zeros((SEL_TQ, SEL_CH), jnp.float32)
        for pair in range(IDX_HEADS // 2):
            q2 = iq_ref[:, pair * LANES:(pair + 1) * LANES]
            for half, k_half in enumerate((k_lo, k_hi)):
                hd = 2 * pair + half
                acc = acc + jnp.maximum(_nt_dot(q2, k_half), 0.0) * iw_ref[:, hd:hd + 1]
        score = jnp.where(c * SEL_CH + lane <= t, acc, NEG_INF)
        key_ref[c] = _sortable(score)
        return carry

    lax.fori_loop(0, n_ch, score_body, 0)

    def count(ref, pred):
        def body(c, acc):
            hit = jnp.where(pred(ref[c]), 1.0, 0.0)
            for j in range(SEL_CH // LANES):
                acc = acc + hit[:, j * LANES:(j + 1) * LANES]
            return acc
        acc = lax.fori_loop(0, n_ch, body, jnp.zeros((SEL_TQ, LANES), jnp.float32))
        return jnp.sum(acc, axis=-1, keepdims=True)

    v = jnp.where(count(key_ref, lambda x: x >= 0) >= n_top, jnp.int32(0), jnp.int32(INT_MIN))

    def bit_body(i, v):
        trial = v + jnp.left_shift(jnp.int32(1), 30 - i)
        return jnp.where(count(key_ref, lambda x: x >= trial) >= n_top, trial, v)

    v = lax.fori_loop(0, 31, bit_body, v)

    need = n_top - count(key_ref, lambda x: x > v)
    far = jnp.int32(2 ** 30)

    def eq_body(c, carry):
        eqidx_ref[c] = jnp.where(key_ref[c] == v, c * SEL_CH + lane, far)
        return carry

    lax.fori_loop(0, n_ch, eq_body, 0)
    nbits = max(1, (key_ref.shape[0] * SEL_CH - 1).bit_length())

    def idx_body(i, cut):
        trial = cut + jnp.left_shift(jnp.int32(1), nbits - 1 - i)
        return jnp.where(count(eqidx_ref, lambda x: x < trial) <= need - 1.0, trial, cut)

    cut = lax.fori_loop(0, nbits, idx_body, jnp.zeros((SEL_TQ, 1), jnp.int32))

    o_ref[...] = jnp.full_like(o_ref, NEG_INF)

    def out_body(c, carry):
        picked = jnp.where(key_ref[c] > v, 0.0, jnp.where(eqidx_ref[c] <= cut, 0.0, NEG_INF))
        o_ref[0, c] = jnp.where(c * SEL_CH + lane <= t, picked, NEG_INF)
        return carry

    lax.fori_loop(0, n_ch, out_body, 0)


def _dsa_select(proj, ik_lo, ik_hi, iw, batch, seq):
    assert seq % SEL_CH == 0 and seq >= SEL_CH >= DSA_TOPK
    n_top = min(DSA_TOPK, seq // 4)
    nq = seq // SEL_TQ
    n_ch = seq // SEL_CH
    iq_w = IDX_HEADS * IDX_DIM
    key_spec = pl.BlockSpec((seq, LANES), lambda b, i: (b, 0))
    return pl.pallas_call(
        functools.partial(_dsa_select_kernel, n_top=n_top),
        grid=(batch, nq),
        in_specs=[pl.BlockSpec((SEL_TQ, iq_w), lambda b, i: (b * nq + i, OFF_IQ // iq_w)),
                  key_spec, key_spec,
                  pl.BlockSpec((SEL_TQ, LANES), lambda b, i: (b * nq + i, 0))],
        out_specs=pl.BlockSpec((1, n_ch, SEL_TQ, SEL_CH), lambda b, i: (b, 0, i, 0)),
        out_shape=jax.ShapeDtypeStruct((batch, n_ch, seq, SEL_CH), jnp.float32),
        scratch_shapes=[pltpu.VMEM((n_ch, SEL_TQ, SEL_CH), jnp.int32), pltpu.VMEM((n_ch, SEL_TQ, SEL_CH), jnp.int32)],
        compiler_params=_params("parallel", "arbitrary"),
        name="dsa_select",
    )(proj, ik_lo, ik_hi, iw)


def _dsa_attn_kernel(q_ref, k_ref, v_ref, bias_ref, mask_ref, o_ref, m_ref, acc_ref):
    qi = pl.program_id(2)
    _init_state(m_ref, acc_ref)
    q = q_ref[...]

    def tile(k0, ntiles, bias_slot):
        rows = _key_rows(k0, ntiles)
        mask = mask_ref[0, k0] if ntiles == 1 else jnp.concatenate(
            [mask_ref[0, k0 + j] for j in range(ntiles)], axis=1)
        z = _nt_dot(q, k_ref[rows, :]) + mask
        if bias_slot is not None:
            z = z + bias_ref[0, bias_slot]
        _softmax_tile(z, _with_ones(v_ref[rows, :]), m_ref, acc_ref)

    _causal_key_loop(qi, tile)
    o_ref[...] = _normalized(acc_ref, HEAD_DIM).astype(o_ref.dtype)


def _dsa_attention(proj, k_dsa, v_dsa, mask, bias_tiles, col0, batch, seq):
    nq = seq // TQ
    kv_spec = pl.BlockSpec((seq, HEAD_DIM), lambda b, h, i: (b, h))
    return pl.pallas_call(
        _dsa_attn_kernel,
        grid=(batch, DSA_HEADS, nq),
        in_specs=[
            pl.BlockSpec((TQ, HEAD_DIM), lambda b, h, i: (b * nq + i, OFF_CQ // HEAD_DIM + h)),
            kv_spec, kv_spec,
            pl.BlockSpec((1, 2, TQ, TK), lambda b, h, i: (col0 + h, 0, 0, 0)),
            pl.BlockSpec((1, seq // TK, TQ, TK), lambda b, h, i: (b, 0, i, 0)),
        ],
        out_specs=pl.BlockSpec((TQ, HEAD_DIM), lambda b, h, i: (b * nq + i, h)),
        out_shape=jax.ShapeDtypeStruct((batch * seq, DSA_HEADS * HEAD_DIM), jnp.bfloat16),
        scratch_shapes=[pltpu.VMEM((TQ, LANES), jnp.float32), pltpu.VMEM((TQ, HEAD_DIM + LANES), jnp.float32)],
        compiler_params=_params("parallel", "parallel", "arbitrary"),
        name="dsa_attention",
    )(proj, k_dsa, v_dsa, bias_tiles, mask)


def _mixers(proj, tail, bias_tiles, diff_lambda, diff_subln_g, kv_norm_g, w_uk, w_uv,
            layer_idx, batch, seq):
    lambda_init = 0.8 - 0.6 * math.exp(-0.3 * layer_idx)
    c0 = 2 * DIFF_HEADS
    c1 = c0 + MOBA_HEADS
    flat = lambda w: w.transpose(1, 0, 2).reshape(KV_LATENT, DSA_HEADS * HEAD_DIM).astype(jnp.bfloat16)
    y_diff = _diff_attention(proj, bias_tiles, diff_lambda, diff_subln_g, lambda_init, batch, seq)
    y_moba = _moba_attention(proj, bias_tiles, c0, batch, seq)
    k_dsa, v_dsa, ik_lo, ik_hi, iw = _dsa_prep(tail, kv_norm_g, flat(w_uk), flat(w_uv))
    mask = _dsa_select(proj, ik_lo, ik_hi, iw, batch, seq)
    y_dsa = _dsa_attention(proj, k_dsa, v_dsa, mask, bias_tiles, c1, batch, seq)
    return jnp.concatenate([y_diff, y_moba, y_dsa], axis=-1)


def _split_w_in(w_in):
    w_main = w_in[:, :N_MAIN].astype(jnp.bfloat16)
    pad = jnp.zeros((w_in.shape[0], TAIL_W - (D_IN - OFF_IK) - KV_LATENT), w_in.dtype)
    w_tail = jnp.concatenate([w_in[:, OFF_CKV:OFF_IQ], w_in[:, OFF_IK:], pad], axis=1).astype(jnp.bfloat16)
    return w_main, w_tail


def kernel(x, ln_emb_g, ln_emb_b, rel_bias, w_in, diff_lambda, diff_subln_g, kv_norm_g, w_uk, w_uv, w_o,
           ln1_g, ln1_b, w_up, w_down, ln2_g, ln2_b):
    batch, seq, d_model = x.shape
    depth = w_in.shape[0]
    alpha = (2.0 * depth) ** 0.25
    bf16 = jnp.bfloat16
    bias_tiles = _bias_tiles(rel_bias)
    h, hb = _layer_norm(x.reshape(batch * seq, d_model), None, ln_emb_g, ln_emb_b, 1.0)
    for l in range(depth):
        w_main, w_tail = _split_w_in(w_in[l])
        proj = _matmul(hb, w_main, bf16)
        tail = _matmul(hb, w_tail, jnp.float32)
        mix = _mixers(proj, tail, bias_tiles, diff_lambda[l], diff_subln_g[l], kv_norm_g[l],
                      w_uk[l], w_uv[l], l, batch, seq)
        y = _matmul(mix, w_o[l].astype(bf16), jnp.float32)
        h, hb = _layer_norm(h, y, ln1_g[l], ln1_b[l], alpha)
        up = _matmul(hb, w_up[l].astype(bf16), bf16, relu2=True)
        y = _matmul(up, w_down[l].astype(bf16), jnp.float32)
        h, hb = _layer_norm(h, y, ln2_g[l], ln2_b[l], alpha)
    return h.reshape(batch, seq, d_model)
```

```python
import functools
import math

import jax
import jax.numpy as jnp
import numpy as np
from jax import lax
from jax.experimental import pallas as pl
from jax.experimental.pallas import tpu as pltpu

HEAD_DIM = 128
DIFF_HEADS = 6
DIFF_DV = 2 * HEAD_DIM
MOBA_HEADS = 10
DSA_HEADS = 10
MOBA_BLOCK = 256
MOBA_TOPK = 3
DSA_TOPK = 256
KV_LATENT = 512
IDX_HEADS = 32
IDX_DIM = 64
NUM_BUCKETS = 32
MAX_DISTANCE = 128
LN_EPS = 1e-5
RMS_EPS = 1e-5
NEG_INF = -1e30

OFF_DQ = 0
OFF_DK = OFF_DQ + 2 * DIFF_HEADS * HEAD_DIM
OFF_DV = OFF_DK + 2 * DIFF_HEADS * HEAD_DIM
OFF_MQ = OFF_DV + DIFF_HEADS * DIFF_DV
OFF_MK = OFF_MQ + MOBA_HEADS * HEAD_DIM
OFF_MV = OFF_MK + MOBA_HEADS * HEAD_DIM
OFF_CQ = OFF_MV + MOBA_HEADS * HEAD_DIM
OFF_CKV = OFF_CQ + DSA_HEADS * HEAD_DIM
OFF_IQ = OFF_CKV + KV_LATENT
OFF_IK = OFF_IQ + IDX_HEADS * IDX_DIM
OFF_IW = OFF_IK + IDX_DIM
D_IN = OFF_IW + IDX_HEADS
N_MAIN = OFF_IK
TAIL_W = KV_LATENT + 128

LANES = 128
VMEM_LIMIT = 56 * 1024 * 1024
TQ = 256
TK = 256
SEL_TQ = TQ
SEL_CH = TK
INT_MIN = -2 ** 31
FAR_TILES = 4
HEADS_PER_STEP = 2
EXP2_SCALE = HEAD_DIM ** -0.5 * math.log2(math.e)


def _t5_thresholds():
    n = np.arange(0, 4 * MAX_DISTANCE)
    max_exact = NUM_BUCKETS // 2
    nf = np.maximum(n, 1).astype(np.float64)
    large = max_exact + (np.log(nf / max_exact) / math.log(MAX_DISTANCE / max_exact)
                         * (NUM_BUCKETS - max_exact)).astype(np.int32)
    bucket = np.where(n < max_exact, n, np.minimum(large, NUM_BUCKETS - 1))
    return [int(np.argmax(bucket >= b)) for b in range(NUM_BUCKETS)]


T5_THRESHOLDS = _t5_thresholds()
assert T5_THRESHOLDS[-1] <= TK // 2


def _params(*sem):
    return pltpu.CompilerParams(dimension_semantics=sem, vmem_limit_bytes=VMEM_LIMIT)


def _ln_kernel(*refs, scale, has_y):
    if has_y:
        x_ref, y_ref, g_ref, b_ref, o_ref, ob_ref = refs
        z = scale * x_ref[...] + y_ref[...]
    else:
        x_ref, g_ref, b_ref, o_ref, ob_ref = refs
        z = x_ref[...]
    mu = jnp.mean(z, axis=-1, keepdims=True)
    zc = z - mu
    var = jnp.mean(zc * zc, axis=-1, keepdims=True)
    out = zc * lax.rsqrt(var + LN_EPS) * g_ref[...] + b_ref[...]
    o_ref[...] = out
    ob_ref[...] = out.astype(jnp.bfloat16)


def _layer_norm(x, y, g, b, scale, rows=256):
    m, d = x.shape
    row_spec = pl.BlockSpec((rows, d), lambda i: (i, 0))
    vec_spec = pl.BlockSpec((1, d), lambda i: (0, 0))
    has_y = y is not None
    ins = (x, y) if has_y else (x,)
    return pl.pallas_call(
        functools.partial(_ln_kernel, scale=scale, has_y=has_y),
        grid=(m // rows,),
        in_specs=[row_spec] * len(ins) + [vec_spec, vec_spec],
        out_specs=[row_spec, row_spec],
        out_shape=[jax.ShapeDtypeStruct((m, d), jnp.float32), jax.ShapeDtypeStruct((m, d), jnp.bfloat16)],
        compiler_params=_params("parallel"),
        name="layer_norm",
    )(*ins, g.reshape(1, d), b.reshape(1, d))


def _mm_kernel(a_ref, w_ref, o_ref, wb_ref, *, relu2):
    @pl.when(pl.program_id(1) == 0)
    def _():
        wb_ref[...] = w_ref[...].astype(jnp.bfloat16)

    acc = jnp.dot(a_ref[...], wb_ref[...], preferred_element_type=jnp.float32)
    if relu2:
        acc = jnp.maximum(acc, 0.0)
        acc = acc * acc
    o_ref[...] = acc.astype(o_ref.dtype)


def _mm_acc_kernel(a_ref, w_ref, o_ref):
    @pl.when(pl.program_id(2) == 0)
    def _():
        o_ref[...] = jnp.zeros_like(o_ref)

    o_ref[...] += jnp.dot(a_ref[...], w_ref[...], preferred_element_type=jnp.float32)


def _matmul_f32w(a, w, layer, n, out_dtype, relu2=False, tm=1024, tn=512):
    m, k = a.shape
    tn = min(tn, n)
    assert n % tn == 0 and m % tm == 0
    return pl.pallas_call(
        functools.partial(_mm_kernel, relu2=relu2),
        grid=(n // tn, m // tm),
        in_specs=[pl.BlockSpec((tm, k), lambda j, i: (i, 0)),
                  pl.BlockSpec((None, k, tn), lambda j, i: (layer, 0, j))],
        out_specs=pl.BlockSpec((tm, tn), lambda j, i: (i, j)),
        out_shape=jax.ShapeDtypeStruct((m, n), out_dtype),
        scratch_shapes=[pltpu.VMEM((k, tn), jnp.bfloat16)],
        compiler_params=_params("parallel", "arbitrary"),
        name="matmul",
    )(a, w)


def _matmul_acc(a, w, tm=1024, tn=1024, tk=4096):
    m, k = a.shape
    n = w.shape[1]
    return pl.pallas_call(
        _mm_acc_kernel,
        grid=(m // tm, n // tn, k // tk),
        in_specs=[pl.BlockSpec((tm, tk), lambda i, j, l: (i, l)), pl.BlockSpec((tk, tn), lambda i, j, l: (l, j))],
        out_specs=pl.BlockSpec((tm, tn), lambda i, j, l: (i, j)),
        out_shape=jax.ShapeDtypeStruct((m, n), jnp.float32),
        compiler_params=_params("parallel", "parallel", "arbitrary"),
        name="matmul_acc",
    )(a, w)


def _bias_tiles_kernel(tab_ref, o_ref):
    col = pl.program_id(0)
    row = lax.broadcasted_iota(jnp.int32, (TQ, TK), 0)
    key = lax.broadcasted_iota(jnp.int32, (TQ, TK), 1)
    last = tab_ref[NUM_BUCKETS - 1, col]
    for tile, shift in ((0, 0), (1, TK)):
        dist = row - key + shift
        bias = jnp.full((TQ, TK), tab_ref[0, col], jnp.float32)
        for b in range(1, NUM_BUCKETS):
            bias = jnp.where(dist >= T5_THRESHOLDS[b], tab_ref[b, col], bias)
        o_ref[0, tile] = jnp.where(dist >= 0, (bias - last) * (HEAD_DIM ** 0.5), NEG_INF)


def _bias_tiles(rel_bias):
    ncols = rel_bias.shape[1]
    return pl.pallas_call(
        _bias_tiles_kernel,
        grid=(ncols,),
        in_specs=[pl.BlockSpec(memory_space=pltpu.SMEM)],
        out_specs=pl.BlockSpec((1, 2, TQ, TK), lambda c: (c, 0, 0, 0)),
        out_shape=jax.ShapeDtypeStruct((ncols, 2, TQ, TK), jnp.float32),
        compiler_params=_params("parallel"),
        name="t5_bias_tiles",
    )(rel_bias)


def _nt_dot(a, b):
    return lax.dot_general(a, b, (((1,), (1,)), ((), ())), preferred_element_type=jnp.float32)


def _lanes(x, n):
    return x if n == LANES else jnp.concatenate([x] * (n // LANES), axis=1)


def _with_ones(v):
    return jnp.concatenate([v, jnp.ones((v.shape[0], LANES), v.dtype)], axis=1)


def _softmax_tile(z, v_ones, m_ref, acc_ref):
    m_old = m_ref[...]
    m_new = jnp.maximum(m_old, jnp.max(z, axis=-1, keepdims=True))
    alpha = jnp.exp2((m_old - m_new) * EXP2_SCALE)
    p = jnp.exp2((z - _lanes(m_new, z.shape[1])) * EXP2_SCALE)
    acc_ref[...] = _lanes(alpha, acc_ref.shape[1]) * acc_ref[...] + jnp.dot(
        p.astype(jnp.bfloat16), v_ones, preferred_element_type=jnp.float32)
    m_ref[...] = m_new


def _init_state(m_ref, acc_ref):
    m_ref[...] = jnp.full_like(m_ref, NEG_INF)
    acc_ref[...] = jnp.zeros_like(acc_ref)


def _normalized(acc_ref, dv):
    acc = acc_ref[...]
    return acc[:, :dv] / _lanes(acc[:, dv:], dv)


def _key_rows(k0, ntiles):
    return pl.ds(pl.multiple_of(k0 * TK, TK), ntiles * TK)


def _near_bias(bias_ref, col, near):
    return bias_ref[col, 0] if near == 1 else jnp.concatenate([bias_ref[col, 1], bias_ref[col, 0]], axis=1)


def _causal_key_loop(qi, tile_fn):
    n_far = jnp.maximum(qi - 1, 0)
    n_big = n_far // FAR_TILES
    rem = n_far - n_big * FAR_TILES

    def big_body(kb, carry):
        tile_fn(kb * FAR_TILES, FAR_TILES, 0)
        return carry

    lax.fori_loop(0, n_big, big_body, 0)

    @pl.when(rem >= 2)
    def _():
        tile_fn(n_big * FAR_TILES, 2, 0)

    @pl.when(rem % 2 == 1)
    def _():
        tile_fn(n_far - 1, 1, 0)

    @pl.when(qi >= 1)
    def _():
        tile_fn(qi - 1, 2, 2)

    @pl.when(qi == 0)
    def _():
        tile_fn(qi, 1, 1)


def _diff_kernel(q_ref, k_ref, v_ref, bias_ref, lam_ref, g_ref, o_ref, m_ref, acc_ref, *, lambda_init):
    qi = pl.program_id(2)
    nmaps = 2 * HEADS_PER_STEP
    for c in range(nmaps):
        _init_state(m_ref.at[c], acc_ref.at[c])

    def tile(k0, ntiles, near):
        rows = _key_rows(k0, ntiles)
        for hh in range(HEADS_PER_STEP):
            v_ones = _with_ones(v_ref[rows, hh * DIFF_DV:(hh + 1) * DIFF_DV])
            for mp in range(2):
                c = 2 * hh + mp
                cols = slice(c * HEAD_DIM, (c + 1) * HEAD_DIM)
                z = _nt_dot(q_ref[:, cols], k_ref[rows, cols])
                if near:
                    z = z + _near_bias(bias_ref, c, near)
                _softmax_tile(z, v_ones, m_ref.at[c], acc_ref.at[c])

    _causal_key_loop(qi, tile)

    lv = lam_ref[...]
    lam = (jnp.exp(jnp.sum(lv[0:1] * lv[1:2], axis=-1, keepdims=True))
           - jnp.exp(jnp.sum(lv[2:3] * lv[3:4], axis=-1, keepdims=True)) + lambda_init)
    for hh in range(HEADS_PER_STEP):
        o = _normalized(acc_ref.at[2 * hh], DIFF_DV) - lam * _normalized(acc_ref.at[2 * hh + 1], DIFF_DV)
        o = o * lax.rsqrt(jnp.mean(o * o, axis=-1, keepdims=True) + RMS_EPS) * g_ref[...]
        o_ref[:, hh * DIFF_DV:(hh + 1) * DIFF_DV] = (o * (1.0 - lambda_init)).astype(o_ref.dtype)


def _diff_attention(proj, bias_tiles, lam_vecs, subln_g, lambda_init, batch, seq):
    assert DIFF_HEADS % HEADS_PER_STEP == 0
    nq = seq // TQ
    qk_w = 2 * HEADS_PER_STEP * HEAD_DIM
    v_w = HEADS_PER_STEP * DIFF_DV
    return pl.pallas_call(
        functools.partial(_diff_kernel, lambda_init=lambda_init),
        grid=(batch, DIFF_HEADS // HEADS_PER_STEP, nq),
        in_specs=[
            pl.BlockSpec((TQ, qk_w), lambda b, h, i: (b * nq + i, OFF_DQ // qk_w + h)),
            pl.BlockSpec((seq, qk_w), lambda b, h, i: (b, OFF_DK // qk_w + h)),
            pl.BlockSpec((seq, v_w), lambda b, h, i: (b, OFF_DV // v_w + h)),
            pl.BlockSpec((2 * HEADS_PER_STEP, 2, TQ, TK), lambda b, h, i: (h, 0, 0, 0)),
            pl.BlockSpec((4, HEAD_DIM), lambda b, h, i: (0, 0)),
            pl.BlockSpec((1, DIFF_DV), lambda b, h, i: (0, 0)),
        ],
        out_specs=pl.BlockSpec((TQ, v_w), lambda b, h, i: (b * nq + i, h)),
        out_shape=jax.ShapeDtypeStruct((batch * seq, DIFF_HEADS * DIFF_DV), jnp.bfloat16),
        scratch_shapes=[pltpu.VMEM((2 * HEADS_PER_STEP, TQ, LANES), jnp.float32),
                        pltpu.VMEM((2 * HEADS_PER_STEP, TQ, DIFF_DV + LANES), jnp.float32)],
        compiler_params=_params("parallel", "parallel", "arbitrary"),
        name="diff_attention",
    )(proj, proj, proj, bias_tiles, lam_vecs, subln_g.reshape(1, DIFF_DV))


def _moba_block_penalty(q, kmean, qi, nblocks):
    nrows = -(-nblocks // 8) * 8
    kmean_hi = kmean.astype(jnp.bfloat16)
    kmean_lo = (kmean - kmean_hi.astype(jnp.float32)).astype(jnp.bfloat16)
    gate = (_nt_dot(kmean_hi, q) + _nt_dot(kmean_lo, q))[:nrows]
    blk = lax.broadcasted_iota(jnp.int32, (nrows, TQ), 0)
    gate = jnp.where(blk < qi, gate, NEG_INF)
    beaten = jnp.zeros((nrows, TQ), jnp.float32)
    for n in range(nblocks):
        other = gate[n:n + 1, :]
        wins = jnp.where(other > gate, 1.0, jnp.where(other == gate, jnp.where(blk > n, 1.0, 0.0), 0.0))
        beaten = beaten + wins
    past_pen = jnp.where(beaten < MOBA_TOPK, 0.0, NEG_INF)
    own_pen = jnp.where(blk == qi, 0.0, NEG_INF)
    pen_t = jnp.where(blk < qi, past_pen, own_pen)
    pen_t = jnp.concatenate([pen_t, jnp.full((LANES - nrows, TQ), NEG_INF, jnp.float32)], axis=0)
    return pen_t.T


def _moba_kernel(q_ref, k_ref, v_ref, bias_ref, o_ref, m_ref, acc_ref, kmean_ref, *, nblocks):
    qi = pl.program_id(2)
    heads = [slice(hh * HEAD_DIM, (hh + 1) * HEAD_DIM) for hh in range(HEADS_PER_STEP)]

    @pl.when(qi == 0)
    def _():
        kmean_ref[...] = jnp.zeros_like(kmean_ref)
        for hh, cols in enumerate(heads):
            for n in range(nblocks):
                blk = k_ref[n * MOBA_BLOCK:(n + 1) * MOBA_BLOCK, cols].astype(jnp.float32)
                kmean_ref[hh, n:n + 1, :] = jnp.mean(blk, axis=0, keepdims=True)

    q_aug = []
    for hh, cols in enumerate(heads):
        _init_state(m_ref.at[hh], acc_ref.at[hh])
        q = q_ref[:, cols]
        pen = _moba_block_penalty(q, kmean_ref[hh], qi, nblocks)
        q_aug.append(jnp.concatenate([q, pen.astype(jnp.bfloat16)], axis=1))

    def tile(k0, ntiles, near):
        rows = _key_rows(k0, ntiles)
        n = ntiles * TK
        block_of_key = k0 + lax.broadcasted_iota(jnp.int32, (n, LANES), 0) // MOBA_BLOCK
        one_hot = jnp.where(lax.broadcasted_iota(jnp.int32, (n, LANES), 1) == block_of_key, 1.0, 0.0)
        one_hot = one_hot.astype(jnp.bfloat16)
        for hh, cols in enumerate(heads):
            z = _nt_dot(q_aug[hh], jnp.concatenate([k_ref[rows, cols], one_hot], axis=1))
            if near:
                z = z + _near_bias(bias_ref, hh, near)
            _softmax_tile(z, _with_ones(v_ref[rows, cols]), m_ref.at[hh], acc_ref.at[hh])

    _causal_key_loop(qi, tile)
    for hh, cols in enumerate(heads):
        o_ref[:, cols] = _normalized(acc_ref.at[hh], HEAD_DIM).astype(o_ref.dtype)


def _moba_attention(proj, bias_tiles, col0, batch, seq):
    assert TQ == MOBA_BLOCK and TK == MOBA_BLOCK and seq % MOBA_BLOCK == 0
    assert MOBA_HEADS % HEADS_PER_STEP == 0 and col0 % HEADS_PER_STEP == 0
    nq = seq // TQ
    nblocks = seq // MOBA_BLOCK
    assert MOBA_TOPK <= nblocks <= LANES
    w = HEADS_PER_STEP * HEAD_DIM
    kv_spec = lambda off: pl.BlockSpec((seq, w), lambda b, h, i: (b, off // w + h))
    return pl.pallas_call(
        functools.partial(_moba_kernel, nblocks=nblocks),
        grid=(batch, MOBA_HEADS // HEADS_PER_STEP, nq),
        in_specs=[
            pl.BlockSpec((TQ, w), lambda b, h, i: (b * nq + i, OFF_MQ // w + h)),
            kv_spec(OFF_MK), kv_spec(OFF_MV),
            pl.BlockSpec((HEADS_PER_STEP, 2, TQ, TK), lambda b, h, i: (col0 // HEADS_PER_STEP + h, 0, 0, 0)),
        ],
        out_specs=pl.BlockSpec((TQ, w), lambda b, h, i: (b * nq + i, h)),
        out_shape=jax.ShapeDtypeStruct((batch * seq, MOBA_HEADS * HEAD_DIM), jnp.bfloat16),
        scratch_shapes=[pltpu.VMEM((HEADS_PER_STEP, TQ, LANES), jnp.float32),
                        pltpu.VMEM((HEADS_PER_STEP, TQ, HEAD_DIM + LANES), jnp.float32),
                        pltpu.VMEM((HEADS_PER_STEP, LANES, HEAD_DIM), jnp.float32)],
        compiler_params=_params("parallel", "parallel", "arbitrary"),
        name="moba_attention",
    )(proj, proj, proj, bias_tiles)


def _dsa_prep_kernel(x_ref, g_ref, wuk_ref, wuv_ref, k_ref, v_ref, iklo_ref, ikhi_ref, iw_ref):
    ckv = x_ref[:, :KV_LATENT]
    ckv = ckv * lax.rsqrt(jnp.mean(ckv * ckv, axis=-1, keepdims=True) + RMS_EPS) * g_ref[...]
    ckv = ckv.astype(jnp.bfloat16)
    k_ref[...] = jnp.dot(ckv, wuk_ref[...], preferred_element_type=jnp.float32).astype(k_ref.dtype)
    v_ref[...] = jnp.dot(ckv, wuv_ref[...], preferred_element_type=jnp.float32).astype(v_ref.dtype)

    grp = x_ref[:, KV_LATENT:]
    lane = lax.broadcasted_iota(jnp.int32, grp.shape, 1)
    is_key = lane < IDX_DIM
    mu = jnp.sum(jnp.where(is_key, grp, 0.0), axis=-1, keepdims=True) / IDX_DIM
    cen = jnp.where(is_key, grp - mu, 0.0)
    var = jnp.sum(cen * cen, axis=-1, keepdims=True) / IDX_DIM
    key_lo = cen * lax.rsqrt(var + LN_EPS)
    iklo_ref[...] = key_lo.astype(iklo_ref.dtype)
    ikhi_ref[...] = pltpu.roll(key_lo, IDX_DIM, axis=1).astype(ikhi_ref.dtype)
    w = pltpu.roll(grp, LANES - IDX_DIM, axis=1)
    iw_ref[...] = jnp.where(lane < IDX_HEADS, w * (IDX_HEADS ** -0.5) * (IDX_DIM ** -0.5), 0.0)


def _dsa_prep(tail, kv_norm_g, wuk_flat, wuv_flat, rows=512):
    m = tail.shape[0]
    n = DSA_HEADS * HEAD_DIM
    row = lambda w: pl.BlockSpec((rows, w), lambda i: (i, 0))
    full = lambda a: pl.BlockSpec(a.shape, lambda i: (0, 0))
    g = kv_norm_g.reshape(1, KV_LATENT)
    return pl.pallas_call(
        _dsa_prep_kernel,
        grid=(m // rows,),
        in_specs=[row(TAIL_W), full(g), full(wuk_flat), full(wuv_flat)],
        out_specs=[row(n), row(n), row(LANES), row(LANES), row(LANES)],
        out_shape=[jax.ShapeDtypeStruct((m, n), jnp.bfloat16), jax.ShapeDtypeStruct((m, n), jnp.bfloat16),
                   jax.ShapeDtypeStruct((m, LANES), jnp.bfloat16), jax.ShapeDtypeStruct((m, LANES), jnp.bfloat16),
                   jax.ShapeDtypeStruct((m, LANES), jnp.float32)],
        compiler_params=_params("parallel"),
        name="dsa_prep",
    )(tail, g, wuk_flat, wuv_flat)


def _sortable(x):
    b = pltpu.bitcast(x, jnp.int32)
    return b ^ ((b >> 31) & jnp.int32(0x7FFFFFFF))


def _dsa_select_kernel(iq_ref, iklo_ref, ikhi_ref, iwt_ref, o_ref, key_ref, eqidx_ref, *, n_top):
    qi = pl.program_id(1)
    n_ch = qi + 1
    shape = (SEL_CH, SEL_TQ)
    t = qi * SEL_TQ + lax.broadcasted_iota(jnp.int32, shape, 1)
    key_in_chunk = lax.broadcasted_iota(jnp.int32, shape, 0)
    chunk = lambda c: pl.ds(pl.multiple_of(c * SEL_CH, SEL_CH), SEL_CH)

    def score_body(c, carry):
        rows = chunk(c)
        k_lo = iklo_ref[rows, :]
        k_hi = ikhi_ref[rows, :]
        acc = jnp.zeros(shape, jnp.float32)
        for pair in range(IDX_HEADS // 2):
            q2 = iq_ref[:, pair * LANES:(pair + 1) * LANES]
            for half, k_half in enumerate((k_lo, k_hi)):
                hd = 2 * pair + half
                acc = acc + jnp.maximum(_nt_dot(k_half, q2), 0.0) * iwt_ref[hd:hd + 1, :]
        score = jnp.where(c * SEL_CH + key_in_chunk <= t, acc, NEG_INF)
        key_ref[c] = _sortable(score)
        return carry

    lax.fori_loop(0, n_ch, score_body, 0)

    def count(ref, pred):
        def body(c, acc):
            hit = jnp.where(pred(ref[c]), 1.0, 0.0)
            return acc + jnp.sum(hit.reshape(SEL_CH // 8, 8, SEL_TQ), axis=0)
        acc = lax.fori_loop(0, n_ch, body, jnp.zeros((8, SEL_TQ), jnp.float32))
        return jnp.sum(acc, axis=0, keepdims=True)

    v = jnp.where(count(key_ref, lambda x: x >= 0) >= n_top, jnp.int32(0), jnp.int32(INT_MIN))

    def bit_body(i, v):
        trial = v + jnp.left_shift(jnp.int32(1), 30 - i)
        return jnp.where(count(key_ref, lambda x: x >= trial) >= n_top, trial, v)

    v = lax.fori_loop(0, 31, bit_body, v)

    need = n_top - count(key_ref, lambda x: x > v)
    far = jnp.int32(2 ** 30)

    def eq_body(c, carry):
        eqidx_ref[c] = jnp.where(key_ref[c] == v, c * SEL_CH + key_in_chunk, far)
        return carry

    lax.fori_loop(0, n_ch, eq_body, 0)
    nbits = max(1, (key_ref.shape[0] * SEL_CH - 1).bit_length())

    def idx_body(i, cut):
        trial = cut + jnp.left_shift(jnp.int32(1), nbits - 1 - i)
        return jnp.where(count(eqidx_ref, lambda x: x < trial) <= need - 1.0, trial, cut)

    cut = lax.fori_loop(0, nbits, idx_body, jnp.zeros((1, SEL_TQ), jnp.int32))

    o_ref[...] = jnp.full_like(o_ref, NEG_INF)

    def out_body(c, carry):
        picked = jnp.where(key_ref[c] > v, 0.0, jnp.where(eqidx_ref[c] <= cut, 0.0, NEG_INF))
        o_ref[0, c] = jnp.where(c * SEL_CH + key_in_chunk <= t, picked, NEG_INF).T
        return carry

    lax.fori_loop(0, n_ch, out_body, 0)


def _dsa_select(proj, ik_lo, ik_hi, iw, batch, seq):
    assert SEL_TQ == SEL_CH and seq % SEL_CH == 0 and SEL_CH >= DSA_TOPK
    n_top = min(DSA_TOPK, seq // 4)
    nq = seq // SEL_TQ
    n_ch = seq // SEL_CH
    iq_w = IDX_HEADS * IDX_DIM
    key_spec = pl.BlockSpec((seq, LANES), lambda b, i: (b, 0))
    iw_t = iw[:, :IDX_HEADS].T
    return pl.pallas_call(
        functools.partial(_dsa_select_kernel, n_top=n_top),
        grid=(batch, nq),
        in_specs=[pl.BlockSpec((SEL_TQ, iq_w), lambda b, i: (b * nq + i, OFF_IQ // iq_w)),
                  key_spec, key_spec,
                  pl.BlockSpec((IDX_HEADS, SEL_TQ), lambda b, i: (0, b * nq + i))],
        out_specs=pl.BlockSpec((1, n_ch, SEL_TQ, SEL_CH), lambda b, i: (b, 0, i, 0)),
        out_shape=jax.ShapeDtypeStruct((batch, n_ch, seq, SEL_CH), jnp.float32),
        scratch_shapes=[pltpu.VMEM((n_ch, SEL_CH, SEL_TQ), jnp.int32), pltpu.VMEM((n_ch, SEL_CH, SEL_TQ), jnp.int32)],
        compiler_params=_params("parallel", "arbitrary"),
        name="dsa_select",
    )(proj, ik_lo, ik_hi, iw_t)


def _dsa_attn_kernel(q_ref, k_ref, v_ref, bias_ref, mask_ref, o_ref, m_ref, acc_ref):
    qi = pl.program_id(2)
    heads = [slice(hh * HEAD_DIM, (hh + 1) * HEAD_DIM) for hh in range(HEADS_PER_STEP)]
    for hh in range(HEADS_PER_STEP):
        _init_state(m_ref.at[hh], acc_ref.at[hh])

    def tile(k0, ntiles, near):
        rows = _key_rows(k0, ntiles)
        mask = mask_ref[0, k0] if ntiles == 1 else jnp.concatenate(
            [mask_ref[0, k0 + j] for j in range(ntiles)], axis=1)
        for hh, cols in enumerate(heads):
            z = _nt_dot(q_ref[:, cols], k_ref[rows, cols]) + mask
            if near:
                z = z + _near_bias(bias_ref, hh, near)
            _softmax_tile(z, _with_ones(v_ref[rows, cols]), m_ref.at[hh], acc_ref.at[hh])

    _causal_key_loop(qi, tile)
    for hh, cols in enumerate(heads):
        o_ref[:, cols] = _normalized(acc_ref.at[hh], HEAD_DIM).astype(o_ref.dtype)


def _dsa_attention(proj, k_dsa, v_dsa, mask, bias_tiles, col0, batch, seq):
    assert DSA_HEADS % HEADS_PER_STEP == 0 and col0 % HEADS_PER_STEP == 0
    nq = seq // TQ
    w = HEADS_PER_STEP * HEAD_DIM
    kv_spec = pl.BlockSpec((seq, w), lambda b, h, i: (b, h))
    return pl.pallas_call(
        _dsa_attn_kernel,
        grid=(batch, DSA_HEADS // HEADS_PER_STEP, nq),
        in_specs=[
            pl.BlockSpec((TQ, w), lambda b, h, i: (b * nq + i, OFF_CQ // w + h)),
            kv_spec, kv_spec,
            pl.BlockSpec((HEADS_PER_STEP, 2, TQ, TK), lambda b, h, i: (col0 // HEADS_PER_STEP + h, 0, 0, 0)),
            pl.BlockSpec((1, seq // TK, TQ, TK), lambda b, h, i: (b, 0, i, 0)),
        ],
        out_specs=pl.BlockSpec((TQ, w), lambda b, h, i: (b * nq + i, h)),
        out_shape=jax.ShapeDtypeStruct((batch * seq, DSA_HEADS * HEAD_DIM), jnp.bfloat16),
        scratch_shapes=[pltpu.VMEM((HEADS_PER_STEP, TQ, LANES), jnp.float32),
                        pltpu.VMEM((HEADS_PER_STEP, TQ, HEAD_DIM + LANES), jnp.float32)],
        compiler_params=_params("parallel", "parallel", "arbitrary"),
        name="dsa_attention",
    )(proj, k_dsa, v_dsa, bias_tiles, mask)


def _mixers(proj, tail, bias_tiles, diff_lambda, diff_subln_g, kv_norm_g, w_uk, w_uv,
            layer_idx, batch, seq):
    lambda_init = 0.8 - 0.6 * math.exp(-0.3 * layer_idx)
    c0 = 2 * DIFF_HEADS
    c1 = c0 + MOBA_HEADS
    flat = lambda w: w.transpose(1, 0, 2).reshape(KV_LATENT, DSA_HEADS * HEAD_DIM).astype(jnp.bfloat16)
    y_diff = _diff_attention(proj, bias_tiles, diff_lambda, diff_subln_g, lambda_init, batch, seq)
    y_moba = _moba_attention(proj, bias_tiles, c0, batch, seq)
    k_dsa, v_dsa, ik_lo, ik_hi, iw = _dsa_prep(tail, kv_norm_g, flat(w_uk), flat(w_uv))
    mask = _dsa_select(proj, ik_lo, ik_hi, iw, batch, seq)
    y_dsa = _dsa_attention(proj, k_dsa, v_dsa, mask, bias_tiles, c1, batch, seq)
    return jnp.concatenate([y_diff, y_moba, y_dsa], axis=-1)


def _tail_weights(w_in):
    pad = jnp.zeros((w_in.shape[0], TAIL_W - (D_IN - OFF_IK) - KV_LATENT), w_in.dtype)
    return jnp.concatenate([w_in[:, OFF_CKV:OFF_IQ], w_in[:, OFF_IK:], pad], axis=1)[None]


def kernel(x, ln_emb_g, ln_emb_b, rel_bias, w_in, diff_lambda, diff_subln_g, kv_norm_g, w_uk, w_uv, w_o,
           ln1_g, ln1_b, w_up, w_down, ln2_g, ln2_b):
    batch, seq, d_model = x.shape
    depth = w_in.shape[0]
    alpha = (2.0 * depth) ** 0.25
    bf16 = jnp.bfloat16
    bias_tiles = _bias_tiles(rel_bias)
    h, hb = _layer_norm(x.reshape(batch * seq, d_model), None, ln_emb_g, ln_emb_b, 1.0)
    for l in range(depth):
        proj = _matmul_f32w(hb, w_in, l, N_MAIN, bf16)
        tail = _matmul_f32w(hb, _tail_weights(w_in[l]), 0, TAIL_W, jnp.float32, tn=TAIL_W)
        mix = _mixers(proj, tail, bias_tiles, diff_lambda[l], diff_subln_g[l], kv_norm_g[l],
                      w_uk[l], w_uv[l], l, batch, seq)
        y = _matmul_f32w(mix, w_o, l, d_model, jnp.float32)
        h, hb = _layer_norm(h, y, ln1_g[l], ln1_b[l], alpha)
        up = _matmul_f32w(hb, w_up, l, w_up.shape[2], bf16, relu2=True)
        y = _matmul_acc(up, w_down[l].astype(bf16))
        h, hb = _layer_norm(h, y, ln2_g[l], ln2_b[l], alpha)
    return h.reshape(batch, seq, d_model)
```

```python
import functools
import math

import jax
import jax.numpy as jnp
import numpy as np
from jax import lax
from jax.experimental import pallas as pl
from jax.experimental.pallas import tpu as pltpu

HEAD_DIM = 128
DIFF_HEADS = 6
DIFF_DV = 2 * HEAD_DIM
MOBA_HEADS = 10
DSA_HEADS = 10
MOBA_BLOCK = 256
MOBA_TOPK = 3
DSA_TOPK = 256
KV_LATENT = 512
IDX_HEADS = 32
IDX_DIM = 64
NUM_BUCKETS = 32
MAX_DISTANCE = 128
LN_EPS = 1e-5
RMS_EPS = 1e-5
NEG_INF = -1e30

OFF_DQ = 0
OFF_DK = OFF_DQ + 2 * DIFF_HEADS * HEAD_DIM
OFF_DV = OFF_DK + 2 * DIFF_HEADS * HEAD_DIM
OFF_MQ = OFF_DV + DIFF_HEADS * DIFF_DV
OFF_MK = OFF_MQ + MOBA_HEADS * HEAD_DIM
OFF_MV = OFF_MK + MOBA_HEADS * HEAD_DIM
OFF_CQ = OFF_MV + MOBA_HEADS * HEAD_DIM
OFF_CKV = OFF_CQ + DSA_HEADS * HEAD_DIM
OFF_IQ = OFF_CKV + KV_LATENT
OFF_IK = OFF_IQ + IDX_HEADS * IDX_DIM
OFF_IW = OFF_IK + IDX_DIM
D_IN = OFF_IW + IDX_HEADS
N_MAIN = OFF_IK
TAIL_W = KV_LATENT + 128

LANES = 128
VMEM_LIMIT = 56 * 1024 * 1024
TQ = 256
TK = 256
SEL_TQ = TQ
SEL_CH = TK
INT_MIN = -2 ** 31
FAR_TILES = 4
HEADS_PER_STEP = 5
DIFF_HEADS_PER_STEP = 3
EXP2_SCALE = HEAD_DIM ** -0.5 * math.log2(math.e)


def _t5_thresholds():
    n = np.arange(0, 4 * MAX_DISTANCE)
    max_exact = NUM_BUCKETS // 2
    nf = np.maximum(n, 1).astype(np.float64)
    large = max_exact + (np.log(nf / max_exact) / math.log(MAX_DISTANCE / max_exact)
                         * (NUM_BUCKETS - max_exact)).astype(np.int32)
    bucket = np.where(n < max_exact, n, np.minimum(large, NUM_BUCKETS - 1))
    return [int(np.argmax(bucket >= b)) for b in range(NUM_BUCKETS)]


T5_THRESHOLDS = _t5_thresholds()
assert T5_THRESHOLDS[-1] <= TK // 2


def _params(*sem):
    return pltpu.CompilerParams(dimension_semantics=sem, vmem_limit_bytes=VMEM_LIMIT)


def _ln_kernel(*refs, scale, has_y):
    if has_y:
        x_ref, y_ref, g_ref, b_ref, o_ref, ob_ref = refs
        z = scale * x_ref[...] + y_ref[...]
    else:
        x_ref, g_ref, b_ref, o_ref, ob_ref = refs
        z = x_ref[...]
    mu = jnp.mean(z, axis=-1, keepdims=True)
    zc = z - mu
    var = jnp.mean(zc * zc, axis=-1, keepdims=True)
    out = zc * lax.rsqrt(var + LN_EPS) * g_ref[...] + b_ref[...]
    o_ref[...] = out
    ob_ref[...] = out.astype(jnp.bfloat16)


def _layer_norm(x, y, g, b, scale, rows=256):
    m, d = x.shape
    row_spec = pl.BlockSpec((rows, d), lambda i: (i, 0))
    vec_spec = pl.BlockSpec((1, d), lambda i: (0, 0))
    has_y = y is not None
    ins = (x, y) if has_y else (x,)
    return pl.pallas_call(
        functools.partial(_ln_kernel, scale=scale, has_y=has_y),
        grid=(m // rows,),
        in_specs=[row_spec] * len(ins) + [vec_spec, vec_spec],
        out_specs=[row_spec, row_spec],
        out_shape=[jax.ShapeDtypeStruct((m, d), jnp.float32), jax.ShapeDtypeStruct((m, d), jnp.bfloat16)],
        compiler_params=_params("parallel"),
        name="layer_norm",
    )(*ins, g.reshape(1, d), b.reshape(1, d))


def _mm_kernel(a_ref, w_ref, o_ref, wb_ref, *, relu2):
    @pl.when(pl.program_id(1) == 0)
    def _():
        wb_ref[...] = w_ref[...].astype(jnp.bfloat16)

    acc = jnp.dot(a_ref[...], wb_ref[...], preferred_element_type=jnp.float32)
    if relu2:
        acc = jnp.maximum(acc, 0.0)
        acc = acc * acc
    o_ref[...] = acc.astype(o_ref.dtype)


def _mm_acc_kernel(a_ref, w_ref, o_ref):
    @pl.when(pl.program_id(2) == 0)
    def _():
        o_ref[...] = jnp.zeros_like(o_ref)

    o_ref[...] += jnp.dot(a_ref[...], w_ref[...], preferred_element_type=jnp.float32)


def _matmul_f32w(a, w, layer, n, out_dtype, relu2=False, tm=1024, tn=512):
    m, k = a.shape
    tn = min(tn, n)
    assert n % tn == 0 and m % tm == 0
    return pl.pallas_call(
        functools.partial(_mm_kernel, relu2=relu2),
        grid=(n // tn, m // tm),
        in_specs=[pl.BlockSpec((tm, k), lambda j, i: (i, 0)),
                  pl.BlockSpec((None, k, tn), lambda j, i: (layer, 0, j))],
        out_specs=pl.BlockSpec((tm, tn), lambda j, i: (i, j)),
        out_shape=jax.ShapeDtypeStruct((m, n), out_dtype),
        scratch_shapes=[pltpu.VMEM((k, tn), jnp.bfloat16)],
        compiler_params=_params("parallel", "arbitrary"),
        name="matmul",
    )(a, w)


def _mm_bf16_kernel(a_ref, w_ref, o_ref):
    o_ref[...] = jnp.dot(a_ref[...], w_ref[...], preferred_element_type=jnp.float32).astype(o_ref.dtype)


def _matmul_bf16w(a, w, out_dtype, tm=1024, tn=1024):
    m, k = a.shape
    n = w.shape[1]
    return pl.pallas_call(
        _mm_bf16_kernel,
        grid=(m // tm, n // tn),
        in_specs=[pl.BlockSpec((tm, k), lambda i, j: (i, 0)), pl.BlockSpec((k, tn), lambda i, j: (0, j))],
        out_specs=pl.BlockSpec((tm, tn), lambda i, j: (i, j)),
        out_shape=jax.ShapeDtypeStruct((m, n), out_dtype),
        compiler_params=_params("parallel", "parallel"),
        name="matmul_bf16w",
    )(a, w)


def _matmul_acc(a, w, tm=1024, tn=1024, tk=4096):
    m, k = a.shape
    n = w.shape[1]
    return pl.pallas_call(
        _mm_acc_kernel,
        grid=(m // tm, n // tn, k // tk),
        in_specs=[pl.BlockSpec((tm, tk), lambda i, j, l: (i, l)), pl.BlockSpec((tk, tn), lambda i, j, l: (l, j))],
        out_specs=pl.BlockSpec((tm, tn), lambda i, j, l: (i, j)),
        out_shape=jax.ShapeDtypeStruct((m, n), jnp.float32),
        compiler_params=_params("parallel", "parallel", "arbitrary"),
        name="matmul_acc",
    )(a, w)


def _bias_tiles_kernel(tab_ref, o_ref):
    col = pl.program_id(0)
    row = lax.broadcasted_iota(jnp.int32, (TQ, TK), 0)
    key = lax.broadcasted_iota(jnp.int32, (TQ, TK), 1)
    last = tab_ref[NUM_BUCKETS - 1, col]
    for tile, shift in ((0, 0), (1, TK)):
        dist = row - key + shift
        bias = jnp.full((TQ, TK), tab_ref[0, col], jnp.float32)
        for b in range(1, NUM_BUCKETS):
            bias = jnp.where(dist >= T5_THRESHOLDS[b], tab_ref[b, col], bias)
        o_ref[0, tile] = jnp.where(dist >= 0, (bias - last) * (HEAD_DIM ** 0.5), NEG_INF)


def _bias_tiles(rel_bias):
    ncols = rel_bias.shape[1]
    return pl.pallas_call(
        _bias_tiles_kernel,
        grid=(ncols,),
        in_specs=[pl.BlockSpec(memory_space=pltpu.SMEM)],
        out_specs=pl.BlockSpec((1, 2, TQ, TK), lambda c: (c, 0, 0, 0)),
        out_shape=jax.ShapeDtypeStruct((ncols, 2, TQ, TK), jnp.float32),
        compiler_params=_params("parallel"),
        name="t5_bias_tiles",
    )(rel_bias)


def _nt_dot(a, b):
    return lax.dot_general(a, b, (((1,), (1,)), ((), ())), preferred_element_type=jnp.float32)


def _lanes(x, n):
    return x if n == LANES else jnp.concatenate([x] * (n // LANES), axis=1)


def _with_ones(v):
    return jnp.concatenate([v, jnp.ones((v.shape[0], LANES), v.dtype)], axis=1)


def _softmax_tile(z, v, m_ref, acc_ref, l_ref=None):
    m_old = m_ref[...]
    m_new = jnp.maximum(m_old, jnp.max(z, axis=-1, keepdims=True))
    alpha = jnp.exp2((m_old - m_new) * EXP2_SCALE)
    p = jnp.exp2((z - _lanes(m_new, z.shape[1])) * EXP2_SCALE)
    if l_ref is not None:
        l_ref[...] = alpha * l_ref[...] + jnp.sum(p, axis=-1, keepdims=True)
    acc_ref[...] = _lanes(alpha, acc_ref.shape[1]) * acc_ref[...] + jnp.dot(
        p.astype(jnp.bfloat16), v, preferred_element_type=jnp.float32)
    m_ref[...] = m_new


def _init_state(m_ref, acc_ref):
    m_ref[...] = jnp.full_like(m_ref, NEG_INF)
    acc_ref[...] = jnp.zeros_like(acc_ref)


def _normalized(acc_ref, dv):
    acc = acc_ref[...]
    return acc[:, :dv] / _lanes(acc[:, dv:], dv)


def _key_rows(k0, ntiles):
    return pl.ds(pl.multiple_of(k0 * TK, TK), ntiles * TK)


def _near_bias(bias_ref, col, near):
    return bias_ref[col, 0] if near == 1 else jnp.concatenate([bias_ref[col, 1], bias_ref[col, 0]], axis=1)


def _causal_key_loop(qi, logits_fn, consume_fn):
    n_far = jnp.maximum(qi - 1, 0)
    n_big = n_far // FAR_TILES
    rem = n_far - n_big * FAR_TILES

    def tile_fn(k0, ntiles, near):
        consume_fn(k0, ntiles, logits_fn(k0, ntiles, near))

    def big_body(kb, carry):
        tile_fn(kb * FAR_TILES, FAR_TILES, 0)
        return carry

    lax.fori_loop(0, n_big, big_body, 0)

    @pl.when(rem >= 2)
    def _():
        tile_fn(n_big * FAR_TILES, 2, 0)

    @pl.when(rem % 2 == 1)
    def _():
        tile_fn(n_far - 1, 1, 0)

    @pl.when(qi >= 1)
    def _():
        tile_fn(qi - 1, 2, 2)

    @pl.when(qi == 0)
    def _():
        tile_fn(qi, 1, 1)


def _diff_kernel(q_ref, k_ref, v_ref, bias_ref, lam_ref, g_ref, o_ref, m_ref, l_ref, acc_ref, *, lambda_init):
    qi = pl.program_id(2)
    nmaps = 2 * DIFF_HEADS_PER_STEP
    for c in range(nmaps):
        _init_state(m_ref.at[c], acc_ref.at[c])
        l_ref[c] = jnp.zeros_like(l_ref[c])

    def logits(k0, ntiles, near):
        rows = _key_rows(k0, ntiles)
        out = []
        for c in range(nmaps):
            cols = slice(c * HEAD_DIM, (c + 1) * HEAD_DIM)
            z = _nt_dot(q_ref[:, cols], k_ref[rows, cols])
            out.append(z + _near_bias(bias_ref, c, near) if near else z)
        return tuple(out)

    def consume(k0, ntiles, zs):
        rows = _key_rows(k0, ntiles)
        for c in range(nmaps):
            hh = c // 2
            _softmax_tile(zs[c], v_ref[rows, hh * DIFF_DV:(hh + 1) * DIFF_DV], m_ref.at[c], acc_ref.at[c],
                          l_ref.at[c])

    _causal_key_loop(qi, logits, consume)

    lv = lam_ref[...]
    lam = (jnp.exp(jnp.sum(lv[0:1] * lv[1:2], axis=-1, keepdims=True))
           - jnp.exp(jnp.sum(lv[2:3] * lv[3:4], axis=-1, keepdims=True)) + lambda_init)
    for hh in range(DIFF_HEADS_PER_STEP):
        a1 = acc_ref[2 * hh] / _lanes(l_ref[2 * hh], DIFF_DV)
        a2 = acc_ref[2 * hh + 1] / _lanes(l_ref[2 * hh + 1], DIFF_DV)
        o = a1 - lam * a2
        o = o * lax.rsqrt(jnp.mean(o * o, axis=-1, keepdims=True) + RMS_EPS) * g_ref[...]
        o_ref[:, hh * DIFF_DV:(hh + 1) * DIFF_DV] = (o * (1.0 - lambda_init)).astype(o_ref.dtype)


def _diff_attention(proj, bias_tiles, lam_vecs, subln_g, lambda_init, batch, seq):
    assert DIFF_HEADS % DIFF_HEADS_PER_STEP == 0
    nq = seq // TQ
    qk_w = 2 * DIFF_HEADS_PER_STEP * HEAD_DIM
    v_w = DIFF_HEADS_PER_STEP * DIFF_DV
    return pl.pallas_call(
        functools.partial(_diff_kernel, lambda_init=lambda_init),
        grid=(batch, DIFF_HEADS // DIFF_HEADS_PER_STEP, nq),
        in_specs=[
            pl.BlockSpec((TQ, qk_w), lambda b, h, i: (b * nq + i, OFF_DQ // qk_w + h)),
            pl.BlockSpec((seq, qk_w), lambda b, h, i: (b, OFF_DK // qk_w + h)),
            pl.BlockSpec((seq, v_w), lambda b, h, i: (b, OFF_DV // v_w + h)),
            pl.BlockSpec((2 * DIFF_HEADS_PER_STEP, 2, TQ, TK), lambda b, h, i: (h, 0, 0, 0)),
            pl.BlockSpec((4, HEAD_DIM), lambda b, h, i: (0, 0)),
            pl.BlockSpec((1, DIFF_DV), lambda b, h, i: (0, 0)),
        ],
        out_specs=pl.BlockSpec((TQ, v_w), lambda b, h, i: (b * nq + i, h)),
        out_shape=jax.ShapeDtypeStruct((batch * seq, DIFF_HEADS * DIFF_DV), jnp.bfloat16),
        scratch_shapes=[pltpu.VMEM((2 * DIFF_HEADS_PER_STEP, TQ, LANES), jnp.float32),
                        pltpu.VMEM((2 * DIFF_HEADS_PER_STEP, TQ, LANES), jnp.float32),
                        pltpu.VMEM((2 * DIFF_HEADS_PER_STEP, TQ, DIFF_DV), jnp.float32)],
        compiler_params=_params("parallel", "parallel", "arbitrary"),
        name="diff_attention",
    )(proj, proj, proj, bias_tiles, lam_vecs, subln_g.reshape(1, DIFF_DV))


def _moba_block_penalty(q, kmean, qi, nblocks):
    nrows = -(-nblocks // 8) * 8
    kmean_hi = kmean.astype(jnp.bfloat16)
    kmean_lo = (kmean - kmean_hi.astype(jnp.float32)).astype(jnp.bfloat16)
    gate = (_nt_dot(kmean_hi, q) + _nt_dot(kmean_lo, q))[:nrows]
    blk = lax.broadcasted_iota(jnp.int32, (nrows, TQ), 0)
    gate = jnp.where(blk < qi, gate, NEG_INF)
    beaten = jnp.zeros((nrows, TQ), jnp.float32)
    for n in range(nblocks):
        other = gate[n:n + 1, :]
        wins = jnp.where(other > gate, 1.0, jnp.where(other == gate, jnp.where(blk > n, 1.0, 0.0), 0.0))
        beaten = beaten + wins
    past_pen = jnp.where(beaten < MOBA_TOPK, 0.0, NEG_INF)
    own_pen = jnp.where(blk == qi, 0.0, NEG_INF)
    pen_t = jnp.where(blk < qi, past_pen, own_pen)
    pen_t = jnp.concatenate([pen_t, jnp.full((LANES - nrows, TQ), NEG_INF, jnp.float32)], axis=0)
    return pen_t.T


def _moba_kernel(*refs, nblocks):
    nh = HEADS_PER_STEP
    q_refs, k_refs, v_refs = refs[:nh], refs[nh:2 * nh], refs[2 * nh:3 * nh]
    bias_ref, o_ref, m_ref, acc_ref, kmean_ref = refs[3 * nh:]
    qi = pl.program_id(2)

    @pl.when(qi == 0)
    def _():
        kmean_ref[...] = jnp.zeros_like(kmean_ref)
        for hh in range(nh):
            for n in range(nblocks):
                blk = k_refs[hh][n * MOBA_BLOCK:(n + 1) * MOBA_BLOCK, :].astype(jnp.float32)
                kmean_ref[hh, n:n + 1, :] = jnp.mean(blk, axis=0, keepdims=True)

    q_aug = []
    for hh in range(nh):
        _init_state(m_ref.at[hh], acc_ref.at[hh])
        q = q_refs[hh][...]
        pen = _moba_block_penalty(q, kmean_ref[hh], qi, nblocks)
        q_aug.append(jnp.concatenate([q, pen.astype(jnp.bfloat16)], axis=1))

    def logits(k0, ntiles, near):
        rows = _key_rows(k0, ntiles)
        n = ntiles * TK
        block_of_key = k0 + lax.broadcasted_iota(jnp.int32, (n, LANES), 0) // MOBA_BLOCK
        one_hot = jnp.where(lax.broadcasted_iota(jnp.int32, (n, LANES), 1) == block_of_key, 1.0, 0.0)
        one_hot = one_hot.astype(jnp.bfloat16)
        out = []
        for hh in range(nh):
            z = _nt_dot(q_aug[hh], jnp.concatenate([k_refs[hh][rows, :], one_hot], axis=1))
            out.append(z + _near_bias(bias_ref, hh, near) if near else z)
        return tuple(out)

    def consume(k0, ntiles, zs):
        rows = _key_rows(k0, ntiles)
        for hh in range(nh):
            _softmax_tile(zs[hh], _with_ones(v_refs[hh][rows, :]), m_ref.at[hh], acc_ref.at[hh])

    _causal_key_loop(qi, logits, consume)
    for hh in range(nh):
        o_ref[:, hh * HEAD_DIM:(hh + 1) * HEAD_DIM] = _normalized(acc_ref.at[hh], HEAD_DIM).astype(o_ref.dtype)


def _head_specs(block_rows, row_index, col_off):
    return [pl.BlockSpec((block_rows, HEAD_DIM),
                         lambda b, h, i, hh=hh: (row_index(b, i), col_off // HEAD_DIM + h * HEADS_PER_STEP + hh))
            for hh in range(HEADS_PER_STEP)]


def _moba_attention(proj, bias_tiles, batch, seq):
    assert TQ == MOBA_BLOCK and TK == MOBA_BLOCK and seq % MOBA_BLOCK == 0
    assert MOBA_HEADS % HEADS_PER_STEP == 0
    nq = seq // TQ
    nblocks = seq // MOBA_BLOCK
    assert MOBA_TOPK <= nblocks <= LANES
    nh = HEADS_PER_STEP
    w = nh * HEAD_DIM
    q_row = lambda b, i: b * nq + i
    seq_row = lambda b, i: b
    return pl.pallas_call(
        functools.partial(_moba_kernel, nblocks=nblocks),
        grid=(batch, MOBA_HEADS // nh, nq),
        in_specs=(_head_specs(TQ, q_row, OFF_MQ) + _head_specs(seq, seq_row, OFF_MK)
                  + _head_specs(seq, seq_row, OFF_MV)
                  + [pl.BlockSpec((nh, 2, TQ, TK), lambda b, h, i: (h, 0, 0, 0))]),
        out_specs=pl.BlockSpec((TQ, w), lambda b, h, i: (b * nq + i, h)),
        out_shape=jax.ShapeDtypeStruct((batch * seq, MOBA_HEADS * HEAD_DIM), jnp.bfloat16),
        scratch_shapes=[pltpu.VMEM((nh, TQ, LANES), jnp.float32),
                        pltpu.VMEM((nh, TQ, HEAD_DIM + LANES), jnp.float32),
                        pltpu.VMEM((nh, LANES, HEAD_DIM), jnp.float32)],
        compiler_params=_params("parallel", "parallel", "arbitrary"),
        name="moba_attention",
    )(*([proj] * (3 * nh)), bias_tiles)


def _dsa_prep_kernel(x_ref, g_ref, wuk_ref, wuv_ref, k_ref, v_ref, iklo_ref, ikhi_ref, iw_ref):
    ckv = x_ref[:, :KV_LATENT]
    ckv = ckv * lax.rsqrt(jnp.mean(ckv * ckv, axis=-1, keepdims=True) + RMS_EPS) * g_ref[...]
    ckv = ckv.astype(jnp.bfloat16)
    k_ref[...] = jnp.dot(ckv, wuk_ref[...], preferred_element_type=jnp.float32).astype(k_ref.dtype)
    v_ref[...] = jnp.dot(ckv, wuv_ref[...], preferred_element_type=jnp.float32).astype(v_ref.dtype)

    grp = x_ref[:, KV_LATENT:]
    lane = lax.broadcasted_iota(jnp.int32, grp.shape, 1)
    is_key = lane < IDX_DIM
    mu = jnp.sum(jnp.where(is_key, grp, 0.0), axis=-1, keepdims=True) / IDX_DIM
    cen = jnp.where(is_key, grp - mu, 0.0)
    var = jnp.sum(cen * cen, axis=-1, keepdims=True) / IDX_DIM
    key_lo = cen * lax.rsqrt(var + LN_EPS)
    iklo_ref[...] = key_lo.astype(iklo_ref.dtype)
    ikhi_ref[...] = pltpu.roll(key_lo, IDX_DIM, axis=1).astype(ikhi_ref.dtype)
    w = pltpu.roll(grp, LANES - IDX_DIM, axis=1)
    iw_ref[...] = jnp.where(lane < IDX_HEADS, w * (IDX_HEADS ** -0.5) * (IDX_DIM ** -0.5), 0.0)


def _dsa_prep(tail, kv_norm_g, wuk_flat, wuv_flat, rows=512):
    m = tail.shape[0]
    n = DSA_HEADS * HEAD_DIM
    row = lambda w: pl.BlockSpec((rows, w), lambda i: (i, 0))
    full = lambda a: pl.BlockSpec(a.shape, lambda i: (0, 0))
    g = kv_norm_g.reshape(1, KV_LATENT)
    return pl.pallas_call(
        _dsa_prep_kernel,
        grid=(m // rows,),
        in_specs=[row(TAIL_W), full(g), full(wuk_flat), full(wuv_flat)],
        out_specs=[row(n), row(n), row(LANES), row(LANES), row(LANES)],
        out_shape=[jax.ShapeDtypeStruct((m, n), jnp.bfloat16), jax.ShapeDtypeStruct((m, n), jnp.bfloat16),
                   jax.ShapeDtypeStruct((m, LANES), jnp.bfloat16), jax.ShapeDtypeStruct((m, LANES), jnp.bfloat16),
                   jax.ShapeDtypeStruct((m, LANES), jnp.float32)],
        compiler_params=_params("parallel"),
        name="dsa_prep",
    )(tail, g, wuk_flat, wuv_flat)


def _sortable(x):
    b = pltpu.bitcast(x, jnp.int32)
    return b ^ ((b >> 31) & jnp.int32(0x7FFFFFFF))


def _dsa_select_kernel(iq_ref, iklo_ref, ikhi_ref, iwt_ref, o_ref, key_ref, eqidx_ref, *, n_top):
    qi = pl.program_id(1)
    n_ch = qi + 1
    shape = (SEL_CH, SEL_TQ)
    t = qi * SEL_TQ + lax.broadcasted_iota(jnp.int32, shape, 1)
    key_in_chunk = lax.broadcasted_iota(jnp.int32, shape, 0)
    chunk = lambda c: pl.ds(pl.multiple_of(c * SEL_CH, SEL_CH), SEL_CH)

    def score_body(c, carry):
        rows = chunk(c)
        k_lo = iklo_ref[rows, :]
        k_hi = ikhi_ref[rows, :]
        acc = jnp.zeros(shape, jnp.float32)
        for pair in range(IDX_HEADS // 2):
            q2 = iq_ref[:, pair * LANES:(pair + 1) * LANES]
            for half, k_half in enumerate((k_lo, k_hi)):
                hd = 2 * pair + half
                acc = acc + jnp.maximum(_nt_dot(k_half, q2), 0.0) * iwt_ref[hd:hd + 1, :]
        score = jnp.where(c * SEL_CH + key_in_chunk <= t, acc, NEG_INF)
        key_ref[c] = _sortable(score)
        return carry

    lax.fori_loop(0, n_ch, score_body, 0)

    def count(ref, pred):
        def body(c, acc):
            hit = jnp.where(pred(ref[c]), 1.0, 0.0)
            return acc + jnp.sum(hit.reshape(SEL_CH // 8, 8, SEL_TQ), axis=0)
        acc = lax.fori_loop(0, n_ch, body, jnp.zeros((8, SEL_TQ), jnp.float32))
        return jnp.sum(acc, axis=0, keepdims=True)

    v = jnp.where(count(key_ref, lambda x: x >= 0) >= n_top, jnp.int32(0), jnp.int32(INT_MIN))

    def bit_body(i, v):
        trial = v + jnp.left_shift(jnp.int32(1), 30 - i)
        return jnp.where(count(key_ref, lambda x: x >= trial) >= n_top, trial, v)

    v = lax.fori_loop(0, 31, bit_body, v)

    need = n_top - count(key_ref, lambda x: x > v)
    far = jnp.int32(2 ** 30)

    def eq_body(c, carry):
        eqidx_ref[c] = jnp.where(key_ref[c] == v, c * SEL_CH + key_in_chunk, far)
        return carry

    lax.fori_loop(0, n_ch, eq_body, 0)
    nbits = max(1, (key_ref.shape[0] * SEL_CH - 1).bit_length())

    def idx_body(i, cut):
        trial = cut + jnp.left_shift(jnp.int32(1), nbits - 1 - i)
        return jnp.where(count(eqidx_ref, lambda x: x < trial) <= need - 1.0, trial, cut)

    cut = lax.fori_loop(0, nbits, idx_body, jnp.zeros((1, SEL_TQ), jnp.int32))

    o_ref[...] = jnp.full_like(o_ref, NEG_INF)

    def out_body(c, carry):
        picked = jnp.where(key_ref[c] > v, 0.0, jnp.where(eqidx_ref[c] <= cut, 0.0, NEG_INF))
        o_ref[0, c] = jnp.where(c * SEL_CH + key_in_chunk <= t, picked, NEG_INF).T
        return carry

    lax.fori_loop(0, n_ch, out_body, 0)


def _dsa_select(proj, ik_lo, ik_hi, iw, batch, seq):
    assert SEL_TQ == SEL_CH and seq % SEL_CH == 0 and SEL_CH >= DSA_TOPK
    n_top = min(DSA_TOPK, seq // 4)
    nq = seq // SEL_TQ
    n_ch = seq // SEL_CH
    iq_w = IDX_HEADS * IDX_DIM
    key_spec = pl.BlockSpec((seq, LANES), lambda b, i: (b, 0))
    iw_t = iw[:, :IDX_HEADS].T
    return pl.pallas_call(
        functools.partial(_dsa_select_kernel, n_top=n_top),
        grid=(batch, nq),
        in_specs=[pl.BlockSpec((SEL_TQ, iq_w), lambda b, i: (b * nq + i, OFF_IQ // iq_w)),
                  key_spec, key_spec,
                  pl.BlockSpec((IDX_HEADS, SEL_TQ), lambda b, i: (0, b * nq + i))],
        out_specs=pl.BlockSpec((1, n_ch, SEL_TQ, SEL_CH), lambda b, i: (b, 0, i, 0)),
        out_shape=jax.ShapeDtypeStruct((batch, n_ch, seq, SEL_CH), jnp.float32),
        scratch_shapes=[pltpu.VMEM((n_ch, SEL_CH, SEL_TQ), jnp.int32), pltpu.VMEM((n_ch, SEL_CH, SEL_TQ), jnp.int32)],
        compiler_params=_params("parallel", "arbitrary"),
        name="dsa_select",
    )(proj, ik_lo, ik_hi, iw_t)


def _dsa_attn_kernel(*refs):
    nh = HEADS_PER_STEP
    q_refs = refs[:nh]
    k_ref, v_ref, bias_ref, mask_ref, o_ref, m_ref, acc_ref = refs[nh:]
    qi = pl.program_id(2)
    heads = [slice(hh * HEAD_DIM, (hh + 1) * HEAD_DIM) for hh in range(nh)]
    for hh in range(nh):
        _init_state(m_ref.at[hh], acc_ref.at[hh])

    def logits(k0, ntiles, near):
        rows = _key_rows(k0, ntiles)
        mask = mask_ref[0, k0] if ntiles == 1 else jnp.concatenate(
            [mask_ref[0, k0 + j] for j in range(ntiles)], axis=1)
        out = []
        for hh, cols in enumerate(heads):
            z = _nt_dot(q_refs[hh][...], k_ref[rows, cols]) + mask
            out.append(z + _near_bias(bias_ref, hh, near) if near else z)
        return tuple(out)

    def consume(k0, ntiles, zs):
        rows = _key_rows(k0, ntiles)
        for hh, cols in enumerate(heads):
            _softmax_tile(zs[hh], _with_ones(v_ref[rows, cols]), m_ref.at[hh], acc_ref.at[hh])

    _causal_key_loop(qi, logits, consume)
    for hh, cols in enumerate(heads):
        o_ref[:, cols] = _normalized(acc_ref.at[hh], HEAD_DIM).astype(o_ref.dtype)


def _dsa_attention(proj, k_dsa, v_dsa, mask, bias_tiles, batch, seq):
    assert DSA_HEADS % HEADS_PER_STEP == 0
    nq = seq // TQ
    nh = HEADS_PER_STEP
    w = nh * HEAD_DIM
    kv_spec = pl.BlockSpec((seq, w), lambda b, h, i: (b, h))
    return pl.pallas_call(
        _dsa_attn_kernel,
        grid=(batch, DSA_HEADS // nh, nq),
        in_specs=(_head_specs(TQ, lambda b, i: b * nq + i, OFF_CQ) + [
            kv_spec, kv_spec,
            pl.BlockSpec((nh, 2, TQ, TK), lambda b, h, i: (h, 0, 0, 0)),
            pl.BlockSpec((1, seq // TK, TQ, TK), lambda b, h, i: (b, 0, i, 0)),
        ]),
        out_specs=pl.BlockSpec((TQ, w), lambda b, h, i: (b * nq + i, h)),
        out_shape=jax.ShapeDtypeStruct((batch * seq, DSA_HEADS * HEAD_DIM), jnp.bfloat16),
        scratch_shapes=[pltpu.VMEM((nh, TQ, LANES), jnp.float32),
                        pltpu.VMEM((nh, TQ, HEAD_DIM + LANES), jnp.float32)],
        compiler_params=_params("parallel", "parallel", "arbitrary"),
        name="dsa_attention",
    )(*([proj] * nh), k_dsa, v_dsa, bias_tiles, mask)


def _mixers(proj, tail, bias_tiles, diff_lambda, diff_subln_g, kv_norm_g, w_uk, w_uv,
            layer_idx, batch, seq):
    lambda_init = 0.8 - 0.6 * math.exp(-0.3 * layer_idx)
    c0 = 2 * DIFF_HEADS
    c1 = c0 + MOBA_HEADS
    flat = lambda w: w.transpose(1, 0, 2).reshape(KV_LATENT, DSA_HEADS * HEAD_DIM).astype(jnp.bfloat16)
    y_diff = _diff_attention(proj, bias_tiles, diff_lambda, diff_subln_g, lambda_init, batch, seq)
    y_moba = _moba_attention(proj, bias_tiles[c0:c1], batch, seq)
    k_dsa, v_dsa, ik_lo, ik_hi, iw = _dsa_prep(tail, kv_norm_g, flat(w_uk), flat(w_uv))
    mask = _dsa_select(proj, ik_lo, ik_hi, iw, batch, seq)
    y_dsa = _dsa_attention(proj, k_dsa, v_dsa, mask, bias_tiles[c1:], batch, seq)
    return jnp.concatenate([y_diff, y_moba, y_dsa], axis=-1)


def _tail_weights(w_in):
    pad = jnp.zeros((w_in.shape[0], TAIL_W - (D_IN - OFF_IK) - KV_LATENT), w_in.dtype)
    return jnp.concatenate([w_in[:, OFF_CKV:OFF_IQ], w_in[:, OFF_IK:], pad], axis=1)[None]


def kernel(x, ln_emb_g, ln_emb_b, rel_bias, w_in, diff_lambda, diff_subln_g, kv_norm_g, w_uk, w_uv, w_o,
           ln1_g, ln1_b, w_up, w_down, ln2_g, ln2_b):
    batch, seq, d_model = x.shape
    depth = w_in.shape[0]
    alpha = (2.0 * depth) ** 0.25
    bf16 = jnp.bfloat16
    bias_tiles = _bias_tiles(rel_bias)
    h, hb = _layer_norm(x.reshape(batch * seq, d_model), None, ln_emb_g, ln_emb_b, 1.0)
    for l in range(depth):
        proj = _matmul_bf16w(hb, w_in[l, :, :N_MAIN].astype(bf16), bf16)
        tail = _matmul_f32w(hb, _tail_weights(w_in[l]), 0, TAIL_W, jnp.float32, tn=TAIL_W)
        mix = _mixers(proj, tail, bias_tiles, diff_lambda[l], diff_subln_g[l], kv_norm_g[l],
                      w_uk[l], w_uv[l], l, batch, seq)
        y = _matmul_f32w(mix, w_o, l, d_model, jnp.float32)
        h, hb = _layer_norm(h, y, ln1_g[l], ln1_b[l], alpha)
        up = _matmul_f32w(hb, w_up, l, w_up.shape[2], bf16, relu2=True)
        y = _matmul_acc(up, w_down[l].astype(bf16))
        h, hb = _layer_norm(h, y, ln2_g[l], ln2_b[l], alpha)
    return h.reshape(batch, seq, d_model)
```

```python
import functools
import math

import jax
import jax.numpy as jnp
import numpy as np
from jax import lax
from jax.experimental import pallas as pl
from jax.experimental.pallas import tpu as pltpu

HEAD_DIM = 128
DIFF_HEADS = 6
DIFF_DV = 2 * HEAD_DIM
MOBA_HEADS = 10
DSA_HEADS = 10
MOBA_BLOCK = 256
MOBA_TOPK = 3
DSA_TOPK = 256
KV_LATENT = 512
IDX_HEADS = 32
IDX_DIM = 64
NUM_BUCKETS = 32
MAX_DISTANCE = 128
LN_EPS = 1e-5
RMS_EPS = 1e-5
NEG_INF = -1e30

OFF_DQ = 0
OFF_DK = OFF_DQ + 2 * DIFF_HEADS * HEAD_DIM
OFF_DV = OFF_DK + 2 * DIFF_HEADS * HEAD_DIM
OFF_MQ = OFF_DV + DIFF_HEADS * DIFF_DV
OFF_MK = OFF_MQ + MOBA_HEADS * HEAD_DIM
OFF_MV = OFF_MK + MOBA_HEADS * HEAD_DIM
OFF_CQ = OFF_MV + MOBA_HEADS * HEAD_DIM
OFF_CKV = OFF_CQ + DSA_HEADS * HEAD_DIM
OFF_IQ = OFF_CKV + KV_LATENT
OFF_IK = OFF_IQ + IDX_HEADS * IDX_DIM
OFF_IW = OFF_IK + IDX_DIM
D_IN = OFF_IW + IDX_HEADS
N_MAIN = OFF_IK
TAIL_W = KV_LATENT + 128

LANES = 128
VMEM_LIMIT = 56 * 1024 * 1024
TQ = 256
TK = 256
SEL_TQ = TQ
SEL_CH = TK
COUNT_ROWS = 32
INT_MIN = -2 ** 31
FAR_TILES = 4
HEADS_PER_STEP = 5
DIFF_HEADS_PER_STEP = 3
EXP2_SCALE = HEAD_DIM ** -0.5 * math.log2(math.e)


def _t5_thresholds():
    n = np.arange(0, 4 * MAX_DISTANCE)
    max_exact = NUM_BUCKETS // 2
    nf = np.maximum(n, 1).astype(np.float64)
    large = max_exact + (np.log(nf / max_exact) / math.log(MAX_DISTANCE / max_exact)
                         * (NUM_BUCKETS - max_exact)).astype(np.int32)
    bucket = np.where(n < max_exact, n, np.minimum(large, NUM_BUCKETS - 1))
    return [int(np.argmax(bucket >= b)) for b in range(NUM_BUCKETS)]


T5_THRESHOLDS = _t5_thresholds()
assert T5_THRESHOLDS[-1] <= TK // 2


def _params(*sem):
    return pltpu.CompilerParams(dimension_semantics=sem, vmem_limit_bytes=VMEM_LIMIT)


def _ln_kernel(*refs, scale, has_y):
    if has_y:
        x_ref, y_ref, g_ref, b_ref, o_ref, ob_ref = refs
        z = scale * x_ref[...] + y_ref[...]
    else:
        x_ref, g_ref, b_ref, o_ref, ob_ref = refs
        z = x_ref[...]
    mu = jnp.mean(z, axis=-1, keepdims=True)
    zc = z - mu
    var = jnp.mean(zc * zc, axis=-1, keepdims=True)
    out = zc * lax.rsqrt(var + LN_EPS) * g_ref[...] + b_ref[...]
    o_ref[...] = out
    ob_ref[...] = out.astype(jnp.bfloat16)


def _layer_norm(x, y, g, b, scale, rows=256):
    m, d = x.shape
    row_spec = pl.BlockSpec((rows, d), lambda i: (i, 0))
    vec_spec = pl.BlockSpec((1, d), lambda i: (0, 0))
    has_y = y is not None
    ins = (x, y) if has_y else (x,)
    return pl.pallas_call(
        functools.partial(_ln_kernel, scale=scale, has_y=has_y),
        grid=(m // rows,),
        in_specs=[row_spec] * len(ins) + [vec_spec, vec_spec],
        out_specs=[row_spec, row_spec],
        out_shape=[jax.ShapeDtypeStruct((m, d), jnp.float32), jax.ShapeDtypeStruct((m, d), jnp.bfloat16)],
        compiler_params=_params("parallel"),
        name="layer_norm",
    )(*ins, g.reshape(1, d), b.reshape(1, d))


def _mm_kernel(a_ref, w_ref, o_ref, wb_ref, *, relu2):
    @pl.when(pl.program_id(1) == 0)
    def _():
        wb_ref[...] = w_ref[...].astype(jnp.bfloat16)

    acc = jnp.dot(a_ref[...], wb_ref[...], preferred_element_type=jnp.float32)
    if relu2:
        acc = jnp.maximum(acc, 0.0)
        acc = acc * acc
    o_ref[...] = acc.astype(o_ref.dtype)


def _mm_acc_kernel(a_ref, w_ref, o_ref):
    @pl.when(pl.program_id(2) == 0)
    def _():
        o_ref[...] = jnp.zeros_like(o_ref)

    o_ref[...] += jnp.dot(a_ref[...], w_ref[...], preferred_element_type=jnp.float32)


def _matmul_f32w(a, w, layer, n, out_dtype, relu2=False, tm=1024, tn=512):
    m, k = a.shape
    tn = min(tn, n)
    assert n % tn == 0 and m % tm == 0
    return pl.pallas_call(
        functools.partial(_mm_kernel, relu2=relu2),
        grid=(n // tn, m // tm),
        in_specs=[pl.BlockSpec((tm, k), lambda j, i: (i, 0)),
                  pl.BlockSpec((None, k, tn), lambda j, i: (layer, 0, j))],
        out_specs=pl.BlockSpec((tm, tn), lambda j, i: (i, j)),
        out_shape=jax.ShapeDtypeStruct((m, n), out_dtype),
        scratch_shapes=[pltpu.VMEM((k, tn), jnp.bfloat16)],
        compiler_params=_params("parallel", "arbitrary"),
        name="matmul",
    )(a, w)


def _mm_bf16_kernel(a_ref, w_ref, o_ref):
    o_ref[...] = jnp.dot(a_ref[...], w_ref[...], preferred_element_type=jnp.float32).astype(o_ref.dtype)


def _matmul_bf16w(a, w, out_dtype, tm=1024, tn=1024):
    m, k = a.shape
    n = w.shape[1]
    return pl.pallas_call(
        _mm_bf16_kernel,
        grid=(m // tm, n // tn),
        in_specs=[pl.BlockSpec((tm, k), lambda i, j: (i, 0)), pl.BlockSpec((k, tn), lambda i, j: (0, j))],
        out_specs=pl.BlockSpec((tm, tn), lambda i, j: (i, j)),
        out_shape=jax.ShapeDtypeStruct((m, n), out_dtype),
        compiler_params=_params("parallel", "parallel"),
        name="matmul_bf16w",
    )(a, w)


def _matmul_acc(a, w, tm=1024, tn=1024, tk=4096):
    m, k = a.shape
    n = w.shape[1]
    return pl.pallas_call(
        _mm_acc_kernel,
        grid=(m // tm, n // tn, k // tk),
        in_specs=[pl.BlockSpec((tm, tk), lambda i, j, l: (i, l)), pl.BlockSpec((tk, tn), lambda i, j, l: (l, j))],
        out_specs=pl.BlockSpec((tm, tn), lambda i, j, l: (i, j)),
        out_shape=jax.ShapeDtypeStruct((m, n), jnp.float32),
        compiler_params=_params("parallel", "parallel", "arbitrary"),
        name="matmul_acc",
    )(a, w)


def _bias_tiles_kernel(tab_ref, o_ref):
    col = pl.program_id(0)
    row = lax.broadcasted_iota(jnp.int32, (TQ, TK), 0)
    key = lax.broadcasted_iota(jnp.int32, (TQ, TK), 1)
    last = tab_ref[NUM_BUCKETS - 1, col]
    for tile, shift in ((0, 0), (1, TK)):
        dist = row - key + shift
        bias = jnp.full((TQ, TK), tab_ref[0, col], jnp.float32)
        for b in range(1, NUM_BUCKETS):
            bias = jnp.where(dist >= T5_THRESHOLDS[b], tab_ref[b, col], bias)
        o_ref[0, tile] = jnp.where(dist >= 0, (bias - last) * (HEAD_DIM ** 0.5), NEG_INF)


def _bias_tiles(rel_bias):
    ncols = rel_bias.shape[1]
    return pl.pallas_call(
        _bias_tiles_kernel,
        grid=(ncols,),
        in_specs=[pl.BlockSpec(memory_space=pltpu.SMEM)],
        out_specs=pl.BlockSpec((1, 2, TQ, TK), lambda c: (c, 0, 0, 0)),
        out_shape=jax.ShapeDtypeStruct((ncols, 2, TQ, TK), jnp.float32),
        compiler_params=_params("parallel"),
        name="t5_bias_tiles",
    )(rel_bias)


def _nt_dot(a, b):
    return lax.dot_general(a, b, (((1,), (1,)), ((), ())), preferred_element_type=jnp.float32)


def _lanes(x, n):
    return x if n == LANES else jnp.concatenate([x] * (n // LANES), axis=1)


def _with_ones(v):
    return jnp.concatenate([v, jnp.ones((v.shape[0], LANES), v.dtype)], axis=1)


def _softmax_tile(z, v, m_ref, acc_ref, l_ref=None):
    m_old = m_ref[...]
    m_new = jnp.maximum(m_old, jnp.max(z, axis=-1, keepdims=True))
    alpha = jnp.exp2((m_old - m_new) * EXP2_SCALE)
    p = jnp.exp2((z - _lanes(m_new, z.shape[1])) * EXP2_SCALE)
    if l_ref is not None:
        l_ref[...] = alpha * l_ref[...] + jnp.sum(p, axis=-1, keepdims=True)
    acc_ref[...] = _lanes(alpha, acc_ref.shape[1]) * acc_ref[...] + jnp.dot(
        p.astype(jnp.bfloat16), v, preferred_element_type=jnp.float32)
    m_ref[...] = m_new


def _init_state(m_ref, acc_ref):
    m_ref[...] = jnp.full_like(m_ref, NEG_INF)
    acc_ref[...] = jnp.zeros_like(acc_ref)


def _normalized(acc_ref, dv):
    acc = acc_ref[...]
    return acc[:, :dv] / _lanes(acc[:, dv:], dv)


def _key_rows(k0, ntiles):
    return pl.ds(pl.multiple_of(k0 * TK, TK), ntiles * TK)


def _near_bias(bias_ref, col, near):
    return bias_ref[col, 0] if near == 1 else jnp.concatenate([bias_ref[col, 1], bias_ref[col, 0]], axis=1)


def _causal_key_loop(qi, logits_fn, consume_fn):
    n_far = jnp.maximum(qi - 1, 0)
    n_big = n_far // FAR_TILES
    rem = n_far - n_big * FAR_TILES

    def tile_fn(k0, ntiles, near):
        consume_fn(k0, ntiles, logits_fn(k0, ntiles, near))

    def big_body(kb, carry):
        tile_fn(kb * FAR_TILES, FAR_TILES, 0)
        return carry

    lax.fori_loop(0, n_big, big_body, 0)

    @pl.when(rem >= 2)
    def _():
        tile_fn(n_big * FAR_TILES, 2, 0)

    @pl.when(rem % 2 == 1)
    def _():
        tile_fn(n_far - 1, 1, 0)

    @pl.when(qi >= 1)
    def _():
        tile_fn(qi - 1, 2, 2)

    @pl.when(qi == 0)
    def _():
        tile_fn(qi, 1, 1)


def _diff_kernel(q_ref, k_ref, v_ref, bias_ref, lam_ref, g_ref, o_ref, m_ref, l_ref, acc_ref, *, lambda_init):
    qi = pl.program_id(2)
    nmaps = 2 * DIFF_HEADS_PER_STEP
    for c in range(nmaps):
        _init_state(m_ref.at[c], acc_ref.at[c])
        l_ref[c] = jnp.zeros_like(l_ref[c])

    def logits(k0, ntiles, near):
        rows = _key_rows(k0, ntiles)
        out = []
        for c in range(nmaps):
            cols = slice(c * HEAD_DIM, (c + 1) * HEAD_DIM)
            z = _nt_dot(q_ref[:, cols], k_ref[rows, cols])
            out.append(z + _near_bias(bias_ref, c, near) if near else z)
        return tuple(out)

    def consume(k0, ntiles, zs):
        rows = _key_rows(k0, ntiles)
        for c in range(nmaps):
            hh = c // 2
            _softmax_tile(zs[c], v_ref[rows, hh * DIFF_DV:(hh + 1) * DIFF_DV], m_ref.at[c], acc_ref.at[c],
                          l_ref.at[c])

    _causal_key_loop(qi, logits, consume)

    lv = lam_ref[...]
    lam = (jnp.exp(jnp.sum(lv[0:1] * lv[1:2], axis=-1, keepdims=True))
           - jnp.exp(jnp.sum(lv[2:3] * lv[3:4], axis=-1, keepdims=True)) + lambda_init)
    for hh in range(DIFF_HEADS_PER_STEP):
        a1 = acc_ref[2 * hh] / _lanes(l_ref[2 * hh], DIFF_DV)
        a2 = acc_ref[2 * hh + 1] / _lanes(l_ref[2 * hh + 1], DIFF_DV)
        o = a1 - lam * a2
        o = o * lax.rsqrt(jnp.mean(o * o, axis=-1, keepdims=True) + RMS_EPS) * g_ref[...]
        o_ref[:, hh * DIFF_DV:(hh + 1) * DIFF_DV] = (o * (1.0 - lambda_init)).astype(o_ref.dtype)


def _diff_attention(proj, bias_tiles, lam_vecs, subln_g, lambda_init, batch, seq):
    assert DIFF_HEADS % DIFF_HEADS_PER_STEP == 0
    nq = seq // TQ
    qk_w = 2 * DIFF_HEADS_PER_STEP * HEAD_DIM
    v_w = DIFF_HEADS_PER_STEP * DIFF_DV
    return pl.pallas_call(
        functools.partial(_diff_kernel, lambda_init=lambda_init),
        grid=(batch, DIFF_HEADS // DIFF_HEADS_PER_STEP, nq),
        in_specs=[
            pl.BlockSpec((TQ, qk_w), lambda b, h, i: (b * nq + i, OFF_DQ // qk_w + h)),
            pl.BlockSpec((seq, qk_w), lambda b, h, i: (b, OFF_DK // qk_w + h)),
            pl.BlockSpec((seq, v_w), lambda b, h, i: (b, OFF_DV // v_w + h)),
            pl.BlockSpec((2 * DIFF_HEADS_PER_STEP, 2, TQ, TK), lambda b, h, i: (h, 0, 0, 0)),
            pl.BlockSpec((4, HEAD_DIM), lambda b, h, i: (0, 0)),
            pl.BlockSpec((1, DIFF_DV), lambda b, h, i: (0, 0)),
        ],
        out_specs=pl.BlockSpec((TQ, v_w), lambda b, h, i: (b * nq + i, h)),
        out_shape=jax.ShapeDtypeStruct((batch * seq, DIFF_HEADS * DIFF_DV), jnp.bfloat16),
        scratch_shapes=[pltpu.VMEM((2 * DIFF_HEADS_PER_STEP, TQ, LANES), jnp.float32),
                        pltpu.VMEM((2 * DIFF_HEADS_PER_STEP, TQ, LANES), jnp.float32),
                        pltpu.VMEM((2 * DIFF_HEADS_PER_STEP, TQ, DIFF_DV), jnp.float32)],
        compiler_params=_params("parallel", "parallel", "arbitrary"),
        name="diff_attention",
    )(proj, proj, proj, bias_tiles, lam_vecs, subln_g.reshape(1, DIFF_DV))


def _moba_block_penalty(q, kmean, qi, nblocks):
    nrows = -(-nblocks // 8) * 8
    kmean_hi = kmean.astype(jnp.bfloat16)
    kmean_lo = (kmean - kmean_hi.astype(jnp.float32)).astype(jnp.bfloat16)
    gate = (_nt_dot(kmean_hi, q) + _nt_dot(kmean_lo, q))[:nrows]
    blk = lax.broadcasted_iota(jnp.int32, (nrows, TQ), 0)
    gate = jnp.where(blk < qi, gate, NEG_INF)
    beaten = jnp.zeros((nrows, TQ), jnp.float32)
    for n in range(nblocks):
        other = gate[n:n + 1, :]
        wins = jnp.where(other > gate, 1.0, jnp.where(other == gate, jnp.where(blk > n, 1.0, 0.0), 0.0))
        beaten = beaten + wins
    past_pen = jnp.where(beaten < MOBA_TOPK, 0.0, NEG_INF)
    own_pen = jnp.where(blk == qi, 0.0, NEG_INF)
    pen_t = jnp.where(blk < qi, past_pen, own_pen)
    pen_t = jnp.concatenate([pen_t, jnp.full((LANES - nrows, TQ), NEG_INF, jnp.float32)], axis=0)
    return pen_t.T


def _moba_kernel(*refs, nblocks):
    nh = HEADS_PER_STEP
    q_refs, k_refs, v_refs = refs[:nh], refs[nh:2 * nh], refs[2 * nh:3 * nh]
    bias_ref, o_ref, m_ref, acc_ref, kmean_ref = refs[3 * nh:]
    qi = pl.program_id(2)

    @pl.when(qi == 0)
    def _():
        kmean_ref[...] = jnp.zeros_like(kmean_ref)
        for hh in range(nh):
            for n in range(nblocks):
                blk = k_refs[hh][n * MOBA_BLOCK:(n + 1) * MOBA_BLOCK, :].astype(jnp.float32)
                kmean_ref[hh, n:n + 1, :] = jnp.mean(blk, axis=0, keepdims=True)

    q_aug = []
    for hh in range(nh):
        _init_state(m_ref.at[hh], acc_ref.at[hh])
        q = q_refs[hh][...]
        pen = _moba_block_penalty(q, kmean_ref[hh], qi, nblocks)
        q_aug.append(jnp.concatenate([q, pen.astype(jnp.bfloat16)], axis=1))

    def logits(k0, ntiles, near):
        rows = _key_rows(k0, ntiles)
        n = ntiles * TK
        block_of_key = k0 + lax.broadcasted_iota(jnp.int32, (n, LANES), 0) // MOBA_BLOCK
        one_hot = jnp.where(lax.broadcasted_iota(jnp.int32, (n, LANES), 1) == block_of_key, 1.0, 0.0)
        one_hot = one_hot.astype(jnp.bfloat16)
        out = []
        for hh in range(nh):
            z = _nt_dot(q_aug[hh], jnp.concatenate([k_refs[hh][rows, :], one_hot], axis=1))
            out.append(z + _near_bias(bias_ref, hh, near) if near else z)
        return tuple(out)

    def consume(k0, ntiles, zs):
        rows = _key_rows(k0, ntiles)
        for hh in range(nh):
            _softmax_tile(zs[hh], _with_ones(v_refs[hh][rows, :]), m_ref.at[hh], acc_ref.at[hh])

    _causal_key_loop(qi, logits, consume)
    for hh in range(nh):
        o_ref[:, hh * HEAD_DIM:(hh + 1) * HEAD_DIM] = _normalized(acc_ref.at[hh], HEAD_DIM).astype(o_ref.dtype)


def _head_specs(block_rows, row_index, col_off):
    return [pl.BlockSpec((block_rows, HEAD_DIM),
                         lambda b, h, i, hh=hh: (row_index(b, i), col_off // HEAD_DIM + h * HEADS_PER_STEP + hh))
            for hh in range(HEADS_PER_STEP)]


def _moba_attention(proj, bias_tiles, batch, seq):
    assert TQ == MOBA_BLOCK and TK == MOBA_BLOCK and seq % MOBA_BLOCK == 0
    assert MOBA_HEADS % HEADS_PER_STEP == 0
    nq = seq // TQ
    nblocks = seq // MOBA_BLOCK
    assert MOBA_TOPK <= nblocks <= LANES
    nh = HEADS_PER_STEP
    w = nh * HEAD_DIM
    q_row = lambda b, i: b * nq + i
    seq_row = lambda b, i: b
    return pl.pallas_call(
        functools.partial(_moba_kernel, nblocks=nblocks),
        grid=(batch, MOBA_HEADS // nh, nq),
        in_specs=(_head_specs(TQ, q_row, OFF_MQ) + _head_specs(seq, seq_row, OFF_MK)
                  + _head_specs(seq, seq_row, OFF_MV)
                  + [pl.BlockSpec((nh, 2, TQ, TK), lambda b, h, i: (h, 0, 0, 0))]),
        out_specs=pl.BlockSpec((TQ, w), lambda b, h, i: (b * nq + i, h)),
        out_shape=jax.ShapeDtypeStruct((batch * seq, MOBA_HEADS * HEAD_DIM), jnp.bfloat16),
        scratch_shapes=[pltpu.VMEM((nh, TQ, LANES), jnp.float32),
                        pltpu.VMEM((nh, TQ, HEAD_DIM + LANES), jnp.float32),
                        pltpu.VMEM((nh, LANES, HEAD_DIM), jnp.float32)],
        compiler_params=_params("parallel", "parallel", "arbitrary"),
        name="moba_attention",
    )(*([proj] * (3 * nh)), bias_tiles)


def _dsa_prep_kernel(x_ref, g_ref, wuk_ref, wuv_ref, k_ref, v_ref, iklo_ref, ikhi_ref, iw_ref):
    ckv = x_ref[:, :KV_LATENT]
    ckv = ckv * lax.rsqrt(jnp.mean(ckv * ckv, axis=-1, keepdims=True) + RMS_EPS) * g_ref[...]
    ckv = ckv.astype(jnp.bfloat16)
    k_ref[...] = jnp.dot(ckv, wuk_ref[...], preferred_element_type=jnp.float32).astype(k_ref.dtype)
    v_ref[...] = jnp.dot(ckv, wuv_ref[...], preferred_element_type=jnp.float32).astype(v_ref.dtype)

    grp = x_ref[:, KV_LATENT:]
    lane = lax.broadcasted_iota(jnp.int32, grp.shape, 1)
    is_key = lane < IDX_DIM
    mu = jnp.sum(jnp.where(is_key, grp, 0.0), axis=-1, keepdims=True) / IDX_DIM
    cen = jnp.where(is_key, grp - mu, 0.0)
    var = jnp.sum(cen * cen, axis=-1, keepdims=True) / IDX_DIM
    key_lo = cen * lax.rsqrt(var + LN_EPS)
    iklo_ref[...] = key_lo.astype(iklo_ref.dtype)
    ikhi_ref[...] = pltpu.roll(key_lo, IDX_DIM, axis=1).astype(ikhi_ref.dtype)
    w = pltpu.roll(grp, LANES - IDX_DIM, axis=1)
    iw_ref[...] = jnp.where(lane < IDX_HEADS, w * (IDX_HEADS ** -0.5) * (IDX_DIM ** -0.5), 0.0)


def _dsa_prep(tail, kv_norm_g, wuk_flat, wuv_flat, rows=512):
    m = tail.shape[0]
    n = DSA_HEADS * HEAD_DIM
    row = lambda w: pl.BlockSpec((rows, w), lambda i: (i, 0))
    full = lambda a: pl.BlockSpec(a.shape, lambda i: (0, 0))
    g = kv_norm_g.reshape(1, KV_LATENT)
    return pl.pallas_call(
        _dsa_prep_kernel,
        grid=(m // rows,),
        in_specs=[row(TAIL_W), full(g), full(wuk_flat), full(wuv_flat)],
        out_specs=[row(n), row(n), row(LANES), row(LANES), row(LANES)],
        out_shape=[jax.ShapeDtypeStruct((m, n), jnp.bfloat16), jax.ShapeDtypeStruct((m, n), jnp.bfloat16),
                   jax.ShapeDtypeStruct((m, LANES), jnp.bfloat16), jax.ShapeDtypeStruct((m, LANES), jnp.bfloat16),
                   jax.ShapeDtypeStruct((m, LANES), jnp.float32)],
        compiler_params=_params("parallel"),
        name="dsa_prep",
    )(tail, g, wuk_flat, wuv_flat)


def _sortable(x):
    b = pltpu.bitcast(x, jnp.int32)
    return b ^ ((b >> 31) & jnp.int32(0x7FFFFFFF))


def _dsa_select_kernel(iq_ref, iklo_ref, ikhi_ref, iwt_ref, o_ref, key_ref, eqidx_ref, cut_ref, *, n_top):
    qi = pl.program_id(1)
    n_ch = qi + 1
    shape = (SEL_CH, SEL_TQ)
    t = qi * SEL_TQ + lax.broadcasted_iota(jnp.int32, shape, 1)
    key_in_chunk = lax.broadcasted_iota(jnp.int32, shape, 0)
    chunk = lambda c: pl.ds(pl.multiple_of(c * SEL_CH, SEL_CH), SEL_CH)

    def score_body(c, carry):
        rows = chunk(c)
        k_lo = iklo_ref[rows, :]
        k_hi = ikhi_ref[rows, :]
        acc = jnp.zeros(shape, jnp.float32)
        for pair in range(IDX_HEADS // 2):
            q2 = iq_ref[:, pair * LANES:(pair + 1) * LANES]
            for half, k_half in enumerate((k_lo, k_hi)):
                hd = 2 * pair + half
                acc = acc + jnp.maximum(_nt_dot(k_half, q2), 0.0) * iwt_ref[hd:hd + 1, :]
        score = jnp.where(c * SEL_CH + key_in_chunk <= t, acc, NEG_INF)
        key_ref[c] = _sortable(score)
        return carry

    lax.fori_loop(0, n_ch, score_body, 0)

    def count(ref, pred):
        def body(c, acc):
            hit = jnp.where(pred(ref[c]), 1.0, 0.0)
            return acc + jnp.sum(hit.reshape(SEL_CH // COUNT_ROWS, COUNT_ROWS, SEL_TQ), axis=0)
        acc = lax.fori_loop(0, n_ch, body, jnp.zeros((COUNT_ROWS, SEL_TQ), jnp.float32))
        return jnp.sum(acc, axis=0, keepdims=True)

    v = jnp.where(count(key_ref, lambda x: x >= 0) >= n_top, jnp.int32(0), jnp.int32(INT_MIN))

    def bit_body(i, v):
        trial = v + jnp.left_shift(jnp.int32(1), 30 - i)
        return jnp.where(count(key_ref, lambda x: x >= trial) >= n_top, trial, v)

    v = lax.fori_loop(0, 31, bit_body, v)

    n_above = count(key_ref, lambda x: x > v)
    need = n_top - n_above
    n_tied = count(key_ref, lambda x: x >= v) - n_above
    far = jnp.int32(2 ** 30)

    def eq_body(c, carry):
        eqidx_ref[c] = jnp.where(key_ref[c] == v, c * SEL_CH + key_in_chunk, far)
        return carry

    lax.fori_loop(0, n_ch, eq_body, 0)
    nbits = max(1, (key_ref.shape[0] * SEL_CH - 1).bit_length())
    cut_ref[...] = jnp.full_like(cut_ref, far - 1)

    @pl.when(jnp.max(jnp.where(n_tied > need, 1.0, 0.0)) > 0.0)
    def _():
        def idx_body(i, cut):
            trial = cut + jnp.left_shift(jnp.int32(1), nbits - 1 - i)
            return jnp.where(count(eqidx_ref, lambda x: x < trial) <= need - 1.0, trial, cut)

        cut_ref[...] = lax.fori_loop(0, nbits, idx_body, jnp.zeros((1, SEL_TQ), jnp.int32))

    cut = cut_ref[...]
    o_ref[...] = jnp.full_like(o_ref, NEG_INF)

    def out_body(c, carry):
        picked = jnp.where(key_ref[c] > v, 0.0, jnp.where(eqidx_ref[c] <= cut, 0.0, NEG_INF))
        o_ref[0, c] = jnp.where(c * SEL_CH + key_in_chunk <= t, picked, NEG_INF).T
        return carry

    lax.fori_loop(0, n_ch, out_body, 0)


def _dsa_select(proj, ik_lo, ik_hi, iw, batch, seq):
    assert SEL_TQ == SEL_CH and seq % SEL_CH == 0 and SEL_CH >= DSA_TOPK
    n_top = min(DSA_TOPK, seq // 4)
    nq = seq // SEL_TQ
    n_ch = seq // SEL_CH
    iq_w = IDX_HEADS * IDX_DIM
    key_spec = pl.BlockSpec((seq, LANES), lambda b, i: (b, 0))
    iw_t = iw[:, :IDX_HEADS].T
    return pl.pallas_call(
        functools.partial(_dsa_select_kernel, n_top=n_top),
        grid=(batch, nq),
        in_specs=[pl.BlockSpec((SEL_TQ, iq_w), lambda b, i: (b * nq + i, OFF_IQ // iq_w)),
                  key_spec, key_spec,
                  pl.BlockSpec((IDX_HEADS, SEL_TQ), lambda b, i: (0, b * nq + i))],
        out_specs=pl.BlockSpec((1, n_ch, SEL_TQ, SEL_CH), lambda b, i: (b, 0, i, 0)),
        out_shape=jax.ShapeDtypeStruct((batch, n_ch, seq, SEL_CH), jnp.float32),
        scratch_shapes=[pltpu.VMEM((n_ch, SEL_CH, SEL_TQ), jnp.int32), pltpu.VMEM((n_ch, SEL_CH, SEL_TQ), jnp.int32),
                        pltpu.VMEM((1, SEL_TQ), jnp.int32)],
        compiler_params=_params("parallel", "arbitrary"),
        name="dsa_select",
    )(proj, ik_lo, ik_hi, iw_t)


def _dsa_attn_kernel(*refs):
    nh = HEADS_PER_STEP
    q_refs = refs[:nh]
    k_ref, v_ref, bias_ref, mask_ref, o_ref, m_ref, acc_ref = refs[nh:]
    qi = pl.program_id(2)
    heads = [slice(hh * HEAD_DIM, (hh + 1) * HEAD_DIM) for hh in range(nh)]
    for hh in range(nh):
        _init_state(m_ref.at[hh], acc_ref.at[hh])

    def logits(k0, ntiles, near):
        rows = _key_rows(k0, ntiles)
        mask = mask_ref[0, k0] if ntiles == 1 else jnp.concatenate(
            [mask_ref[0, k0 + j] for j in range(ntiles)], axis=1)
        out = []
        for hh, cols in enumerate(heads):
            z = _nt_dot(q_refs[hh][...], k_ref[rows, cols]) + mask
            out.append(z + _near_bias(bias_ref, hh, near) if near else z)
        return tuple(out)

    def consume(k0, ntiles, zs):
        rows = _key_rows(k0, ntiles)
        for hh, cols in enumerate(heads):
            _softmax_tile(zs[hh], _with_ones(v_ref[rows, cols]), m_ref.at[hh], acc_ref.at[hh])

    _causal_key_loop(qi, logits, consume)
    for hh, cols in enumerate(heads):
        o_ref[:, cols] = _normalized(acc_ref.at[hh], HEAD_DIM).astype(o_ref.dtype)


def _dsa_attention(proj, k_dsa, v_dsa, mask, bias_tiles, batch, seq):
    assert DSA_HEADS % HEADS_PER_STEP == 0
    nq = seq // TQ
    nh = HEADS_PER_STEP
    w = nh * HEAD_DIM
    kv_spec = pl.BlockSpec((seq, w), lambda b, h, i: (b, h))
    return pl.pallas_call(
        _dsa_attn_kernel,
        grid=(batch, DSA_HEADS // nh, nq),
        in_specs=(_head_specs(TQ, lambda b, i: b * nq + i, OFF_CQ) + [
            kv_spec, kv_spec,
            pl.BlockSpec((nh, 2, TQ, TK), lambda b, h, i: (h, 0, 0, 0)),
            pl.BlockSpec((1, seq // TK, TQ, TK), lambda b, h, i: (b, 0, i, 0)),
        ]),
        out_specs=pl.BlockSpec((TQ, w), lambda b, h, i: (b * nq + i, h)),
        out_shape=jax.ShapeDtypeStruct((batch * seq, DSA_HEADS * HEAD_DIM), jnp.bfloat16),
        scratch_shapes=[pltpu.VMEM((nh, TQ, LANES), jnp.float32),
                        pltpu.VMEM((nh, TQ, HEAD_DIM + LANES), jnp.float32)],
        compiler_params=_params("parallel", "parallel", "arbitrary"),
        name="dsa_attention",
    )(*([proj] * nh), k_dsa, v_dsa, bias_tiles, mask)


def _mixers(proj, tail, bias_tiles, diff_lambda, diff_subln_g, kv_norm_g, w_uk, w_uv,
            layer_idx, batch, seq):
    lambda_init = 0.8 - 0.6 * math.exp(-0.3 * layer_idx)
    c0 = 2 * DIFF_HEADS
    c1 = c0 + MOBA_HEADS
    flat = lambda w: w.transpose(1, 0, 2).reshape(KV_LATENT, DSA_HEADS * HEAD_DIM).astype(jnp.bfloat16)
    y_diff = _diff_attention(proj, bias_tiles, diff_lambda, diff_subln_g, lambda_init, batch, seq)
    y_moba = _moba_attention(proj, bias_tiles[c0:c1], batch, seq)
    k_dsa, v_dsa, ik_lo, ik_hi, iw = _dsa_prep(tail, kv_norm_g, flat(w_uk), flat(w_uv))
    mask = _dsa_select(proj, ik_lo, ik_hi, iw, batch, seq)
    y_dsa = _dsa_attention(proj, k_dsa, v_dsa, mask, bias_tiles[c1:], batch, seq)
    return jnp.concatenate([y_diff, y_moba, y_dsa], axis=-1)


def _split_w_in(w_in):
    w_main = w_in[:, :N_MAIN].astype(jnp.bfloat16)
    pad = jnp.zeros((w_in.shape[0], TAIL_W - (D_IN - OFF_IK) - KV_LATENT), w_in.dtype)
    w_tail = jnp.concatenate([w_in[:, OFF_CKV:OFF_IQ], w_in[:, OFF_IK:], pad], axis=1).astype(jnp.bfloat16)
    return w_main, w_tail


def kernel(x, ln_emb_g, ln_emb_b, rel_bias, w_in, diff_lambda, diff_subln_g, kv_norm_g, w_uk, w_uv, w_o,
           ln1_g, ln1_b, w_up, w_down, ln2_g, ln2_b):
    batch, seq, d_model = x.shape
    depth = w_in.shape[0]
    alpha = (2.0 * depth) ** 0.25
    bf16 = jnp.bfloat16
    bias_tiles = _bias_tiles(rel_bias)
    h, hb = _layer_norm(x.reshape(batch * seq, d_model), None, ln_emb_g, ln_emb_b, 1.0)
    for l in range(depth):
        w_main, w_tail = _split_w_in(w_in[l])
        proj = _matmul_bf16w(hb, w_main, bf16)
        tail = _matmul_bf16w(hb, w_tail, jnp.float32, tn=TAIL_W)
        mix = _mixers(proj, tail, bias_tiles, diff_lambda[l], diff_subln_g[l], kv_norm_g[l],
                      w_uk[l], w_uv[l], l, batch, seq)
        y = _matmul_f32w(mix, w_o, l, d_model, jnp.float32)
        h, hb = _layer_norm(h, y, ln1_g[l], ln1_b[l], alpha)
        up = _matmul_f32w(hb, w_up, l, w_up.shape[2], bf16, relu2=True)
        y = _matmul_acc(up, w_down[l].astype(bf16))
        h, hb = _layer_norm(h, y, ln2_g[l], ln2_b[l], alpha)
    return h.reshape(batch, seq, d_model)
```

```python
import functools
import math

import jax
import jax.numpy as jnp
import numpy as np
from jax import lax
from jax.experimental import pallas as pl
from jax.experimental.pallas import tpu as pltpu

HEAD_DIM = 128
DIFF_HEADS = 6
DIFF_DV = 2 * HEAD_DIM
MOBA_HEADS = 10
DSA_HEADS = 10
MOBA_BLOCK = 256
MOBA_TOPK = 3
DSA_TOPK = 256
KV_LATENT = 512
IDX_HEADS = 32
IDX_DIM = 64
NUM_BUCKETS = 32
MAX_DISTANCE = 128
LN_EPS = 1e-5
RMS_EPS = 1e-5
NEG_INF = -1e30

OFF_DQ = 0
OFF_DK = OFF_DQ + 2 * DIFF_HEADS * HEAD_DIM
OFF_DV = OFF_DK + 2 * DIFF_HEADS * HEAD_DIM
OFF_MQ = OFF_DV + DIFF_HEADS * DIFF_DV
OFF_MK = OFF_MQ + MOBA_HEADS * HEAD_DIM
OFF_MV = OFF_MK + MOBA_HEADS * HEAD_DIM
OFF_CQ = OFF_MV + MOBA_HEADS * HEAD_DIM
OFF_CKV = OFF_CQ + DSA_HEADS * HEAD_DIM
OFF_IQ = OFF_CKV + KV_LATENT
OFF_IK = OFF_IQ + IDX_HEADS * IDX_DIM
OFF_IW = OFF_IK + IDX_DIM
D_IN = OFF_IW + IDX_HEADS
N_MAIN = OFF_IK
TAIL_W = KV_LATENT + 128

LANES = 128
VMEM_LIMIT = 56 * 1024 * 1024
TQ = 256
TK = 256
SEL_TQ = TQ
SEL_CH = TK
COUNT_ROWS = 32
INT_MIN = -2 ** 31
FAR_TILES = 4
HEADS_PER_STEP = 5
DIFF_HEADS_PER_STEP = 3
EXP2_SCALE = HEAD_DIM ** -0.5 * math.log2(math.e)


def _t5_thresholds():
    n = np.arange(0, 4 * MAX_DISTANCE)
    max_exact = NUM_BUCKETS // 2
    nf = np.maximum(n, 1).astype(np.float64)
    large = max_exact + (np.log(nf / max_exact) / math.log(MAX_DISTANCE / max_exact)
                         * (NUM_BUCKETS - max_exact)).astype(np.int32)
    bucket = np.where(n < max_exact, n, np.minimum(large, NUM_BUCKETS - 1))
    return [int(np.argmax(bucket >= b)) for b in range(NUM_BUCKETS)]


T5_THRESHOLDS = _t5_thresholds()
assert T5_THRESHOLDS[-1] <= TK // 2


def _params(*sem):
    return pltpu.CompilerParams(dimension_semantics=sem, vmem_limit_bytes=VMEM_LIMIT)


def _ln_kernel(*refs, scale, has_y):
    if has_y:
        x_ref, y_ref, g_ref, b_ref, o_ref, ob_ref = refs
        z = scale * x_ref[...] + y_ref[...]
    else:
        x_ref, g_ref, b_ref, o_ref, ob_ref = refs
        z = x_ref[...]
    mu = jnp.mean(z, axis=-1, keepdims=True)
    zc = z - mu
    var = jnp.mean(zc * zc, axis=-1, keepdims=True)
    out = zc * lax.rsqrt(var + LN_EPS) * g_ref[...] + b_ref[...]
    o_ref[...] = out
    ob_ref[...] = out.astype(jnp.bfloat16)


def _layer_norm(x, y, g, b, scale, rows=256):
    m, d = x.shape
    row_spec = pl.BlockSpec((rows, d), lambda i: (i, 0))
    vec_spec = pl.BlockSpec((1, d), lambda i: (0, 0))
    has_y = y is not None
    ins = (x, y) if has_y else (x,)
    return pl.pallas_call(
        functools.partial(_ln_kernel, scale=scale, has_y=has_y),
        grid=(m // rows,),
        in_specs=[row_spec] * len(ins) + [vec_spec, vec_spec],
        out_specs=[row_spec, row_spec],
        out_shape=[jax.ShapeDtypeStruct((m, d), jnp.float32), jax.ShapeDtypeStruct((m, d), jnp.bfloat16)],
        compiler_params=_params("parallel"),
        name="layer_norm",
    )(*ins, g.reshape(1, d), b.reshape(1, d))


def _mm_kernel(*refs, relu2):
    *a_refs, w_ref, o_ref, wb_ref = refs

    @pl.when(pl.program_id(1) == 0)
    def _():
        wb_ref[...] = w_ref[...].astype(jnp.bfloat16)

    acc, row = None, 0
    for a_ref in a_refs:
        part = jnp.dot(a_ref[...], wb_ref[row:row + a_ref.shape[1], :], preferred_element_type=jnp.float32)
        acc = part if acc is None else acc + part
        row += a_ref.shape[1]
    if relu2:
        acc = jnp.maximum(acc, 0.0)
        acc = acc * acc
    o_ref[...] = acc.astype(o_ref.dtype)


def _mm_acc_kernel(a_ref, w_ref, o_ref):
    @pl.when(pl.program_id(2) == 0)
    def _():
        o_ref[...] = jnp.zeros_like(o_ref)

    o_ref[...] += jnp.dot(a_ref[...], w_ref[...], preferred_element_type=jnp.float32)


def _matmul_f32w(a_parts, w, layer, n, out_dtype, relu2=False, tm=1024, tn=512):
    m = a_parts[0].shape[0]
    k = sum(a.shape[1] for a in a_parts)
    tn = min(tn, n)
    assert n % tn == 0 and m % tm == 0 and k == w.shape[1]
    return pl.pallas_call(
        functools.partial(_mm_kernel, relu2=relu2),
        grid=(n // tn, m // tm),
        in_specs=([pl.BlockSpec((tm, a.shape[1]), lambda j, i: (i, 0)) for a in a_parts]
                  + [pl.BlockSpec((None, k, tn), lambda j, i: (layer, 0, j))]),
        out_specs=pl.BlockSpec((tm, tn), lambda j, i: (i, j)),
        out_shape=jax.ShapeDtypeStruct((m, n), out_dtype),
        scratch_shapes=[pltpu.VMEM((k, tn), jnp.bfloat16)],
        compiler_params=_params("parallel", "arbitrary"),
        name="matmul",
    )(*a_parts, w)


def _mm_bf16_kernel(a_ref, w_ref, o_ref):
    o_ref[...] = jnp.dot(a_ref[...], w_ref[...], preferred_element_type=jnp.float32).astype(o_ref.dtype)


def _matmul_bf16w(a, w, layer, n, out_dtype, tm=1024, tn=1024):
    m, k = a.shape
    return pl.pallas_call(
        _mm_bf16_kernel,
        grid=(m // tm, n // tn),
        in_specs=[pl.BlockSpec((tm, k), lambda i, j: (i, 0)),
                  pl.BlockSpec((None, k, tn), lambda i, j: (layer, 0, j))],
        out_specs=pl.BlockSpec((tm, tn), lambda i, j: (i, j)),
        out_shape=jax.ShapeDtypeStruct((m, n), out_dtype),
        compiler_params=_params("parallel", "parallel"),
        name="matmul_bf16w",
    )(a, w)


def _matmul_acc(a, w, layer, tm=1024, tn=1024, tk=4096):
    m, k = a.shape
    n = w.shape[2]
    return pl.pallas_call(
        _mm_acc_kernel,
        grid=(m // tm, n // tn, k // tk),
        in_specs=[pl.BlockSpec((tm, tk), lambda i, j, l: (i, l)),
                  pl.BlockSpec((None, tk, tn), lambda i, j, l: (layer, l, j))],
        out_specs=pl.BlockSpec((tm, tn), lambda i, j, l: (i, j)),
        out_shape=jax.ShapeDtypeStruct((m, n), jnp.float32),
        compiler_params=_params("parallel", "parallel", "arbitrary"),
        name="matmul_acc",
    )(a, w)


def _bias_tiles_kernel(tab_ref, o_ref):
    col = pl.program_id(0)
    row = lax.broadcasted_iota(jnp.int32, (TQ, TK), 0)
    key = lax.broadcasted_iota(jnp.int32, (TQ, TK), 1)
    last = tab_ref[NUM_BUCKETS - 1, col]
    for tile, shift in ((0, 0), (1, TK)):
        dist = row - key + shift
        bias = jnp.full((TQ, TK), tab_ref[0, col], jnp.float32)
        for b in range(1, NUM_BUCKETS):
            bias = jnp.where(dist >= T5_THRESHOLDS[b], tab_ref[b, col], bias)
        o_ref[0, tile] = jnp.where(dist >= 0, (bias - last) * (HEAD_DIM ** 0.5), NEG_INF)


def _bias_tiles(rel_bias):
    ncols = rel_bias.shape[1]
    return pl.pallas_call(
        _bias_tiles_kernel,
        grid=(ncols,),
        in_specs=[pl.BlockSpec(memory_space=pltpu.SMEM)],
        out_specs=pl.BlockSpec((1, 2, TQ, TK), lambda c: (c, 0, 0, 0)),
        out_shape=jax.ShapeDtypeStruct((ncols, 2, TQ, TK), jnp.float32),
        compiler_params=_params("parallel"),
        name="t5_bias_tiles",
    )(rel_bias)


def _nt_dot(a, b):
    return lax.dot_general(a, b, (((1,), (1,)), ((), ())), preferred_element_type=jnp.float32)


def _lanes(x, n):
    return x if n == LANES else jnp.concatenate([x] * (n // LANES), axis=1)


def _with_ones(v):
    return jnp.concatenate([v, jnp.ones((v.shape[0], LANES), v.dtype)], axis=1)


def _softmax_tile(z, v, m_ref, acc_ref, l_ref=None):
    m_old = m_ref[...]
    m_new = jnp.maximum(m_old, jnp.max(z, axis=-1, keepdims=True))
    alpha = jnp.exp2((m_old - m_new) * EXP2_SCALE)
    p = jnp.exp2((z - _lanes(m_new, z.shape[1])) * EXP2_SCALE)
    if l_ref is not None:
        l_ref[...] = alpha * l_ref[...] + jnp.sum(p, axis=-1, keepdims=True)
    acc_ref[...] = _lanes(alpha, acc_ref.shape[1]) * acc_ref[...] + jnp.dot(
        p.astype(jnp.bfloat16), v, preferred_element_type=jnp.float32)
    m_ref[...] = m_new


def _init_state(m_ref, acc_ref):
    m_ref[...] = jnp.full_like(m_ref, NEG_INF)
    acc_ref[...] = jnp.zeros_like(acc_ref)


def _normalized(acc_ref, dv):
    acc = acc_ref[...]
    return acc[:, :dv] / _lanes(acc[:, dv:], dv)


def _key_rows(k0, ntiles):
    return pl.ds(pl.multiple_of(k0 * TK, TK), ntiles * TK)


def _near_bias(bias_ref, col, near):
    return bias_ref[col, 0] if near == 1 else jnp.concatenate([bias_ref[col, 1], bias_ref[col, 0]], axis=1)


def _causal_key_loop(qi, logits_fn, consume_fn):
    n_far = jnp.maximum(qi - 1, 0)
    n_big = n_far // FAR_TILES
    rem = n_far - n_big * FAR_TILES

    def tile_fn(k0, ntiles, near):
        consume_fn(k0, ntiles, logits_fn(k0, ntiles, near))

    def big_body(kb, carry):
        tile_fn(kb * FAR_TILES, FAR_TILES, 0)
        return carry

    lax.fori_loop(0, n_big, big_body, 0)

    @pl.when(rem >= 2)
    def _():
        tile_fn(n_big * FAR_TILES, 2, 0)

    @pl.when(rem % 2 == 1)
    def _():
        tile_fn(n_far - 1, 1, 0)

    @pl.when(qi >= 1)
    def _():
        tile_fn(qi - 1, 2, 2)

    @pl.when(qi == 0)
    def _():
        tile_fn(qi, 1, 1)


def _diff_kernel(q_ref, k_ref, v_ref, bias_ref, lam_ref, g_ref, o_ref, m_ref, l_ref, acc_ref, *, lambda_init):
    qi = pl.program_id(2)
    nmaps = 2 * DIFF_HEADS_PER_STEP
    for c in range(nmaps):
        _init_state(m_ref.at[c], acc_ref.at[c])
        l_ref[c] = jnp.zeros_like(l_ref[c])

    def logits(k0, ntiles, near):
        rows = _key_rows(k0, ntiles)
        out = []
        for c in range(nmaps):
            cols = slice(c * HEAD_DIM, (c + 1) * HEAD_DIM)
            z = _nt_dot(q_ref[:, cols], k_ref[rows, cols])
            out.append(z + _near_bias(bias_ref, c, near) if near else z)
        return tuple(out)

    def consume(k0, ntiles, zs):
        rows = _key_rows(k0, ntiles)
        for c in range(nmaps):
            hh = c // 2
            _softmax_tile(zs[c], v_ref[rows, hh * DIFF_DV:(hh + 1) * DIFF_DV], m_ref.at[c], acc_ref.at[c],
                          l_ref.at[c])

    _causal_key_loop(qi, logits, consume)

    lv = lam_ref[...]
    lam = (jnp.exp(jnp.sum(lv[0:1] * lv[1:2], axis=-1, keepdims=True))
           - jnp.exp(jnp.sum(lv[2:3] * lv[3:4], axis=-1, keepdims=True)) + lambda_init)
    for hh in range(DIFF_HEADS_PER_STEP):
        a1 = acc_ref[2 * hh] / _lanes(l_ref[2 * hh], DIFF_DV)
        a2 = acc_ref[2 * hh + 1] / _lanes(l_ref[2 * hh + 1], DIFF_DV)
        o = a1 - lam * a2
        o = o * lax.rsqrt(jnp.mean(o * o, axis=-1, keepdims=True) + RMS_EPS) * g_ref[...]
        o_ref[:, hh * DIFF_DV:(hh + 1) * DIFF_DV] = (o * (1.0 - lambda_init)).astype(o_ref.dtype)


def _diff_attention(proj, bias_tiles, lam_vecs, subln_g, lambda_init, batch, seq):
    assert DIFF_HEADS % DIFF_HEADS_PER_STEP == 0
    nq = seq // TQ
    qk_w = 2 * DIFF_HEADS_PER_STEP * HEAD_DIM
    v_w = DIFF_HEADS_PER_STEP * DIFF_DV
    return pl.pallas_call(
        functools.partial(_diff_kernel, lambda_init=lambda_init),
        grid=(batch, DIFF_HEADS // DIFF_HEADS_PER_STEP, nq),
        in_specs=[
            pl.BlockSpec((TQ, qk_w), lambda b, h, i: (b * nq + i, OFF_DQ // qk_w + h)),
            pl.BlockSpec((seq, qk_w), lambda b, h, i: (b, OFF_DK // qk_w + h)),
            pl.BlockSpec((seq, v_w), lambda b, h, i: (b, OFF_DV // v_w + h)),
            pl.BlockSpec((2 * DIFF_HEADS_PER_STEP, 2, TQ, TK), lambda b, h, i: (h, 0, 0, 0)),
            pl.BlockSpec((4, HEAD_DIM), lambda b, h, i: (0, 0)),
            pl.BlockSpec((1, DIFF_DV), lambda b, h, i: (0, 0)),
        ],
        out_specs=pl.BlockSpec((TQ, v_w), lambda b, h, i: (b * nq + i, h)),
        out_shape=jax.ShapeDtypeStruct((batch * seq, DIFF_HEADS * DIFF_DV), jnp.bfloat16),
        scratch_shapes=[pltpu.VMEM((2 * DIFF_HEADS_PER_STEP, TQ, LANES), jnp.float32),
                        pltpu.VMEM((2 * DIFF_HEADS_PER_STEP, TQ, LANES), jnp.float32),
                        pltpu.VMEM((2 * DIFF_HEADS_PER_STEP, TQ, DIFF_DV), jnp.float32)],
        compiler_params=_params("parallel", "parallel", "arbitrary"),
        name="diff_attention",
    )(proj, proj, proj, bias_tiles, lam_vecs, subln_g.reshape(1, DIFF_DV))


def _moba_block_penalty(q, kmean, qi, nblocks):
    nrows = -(-nblocks // 8) * 8
    kmean_hi = kmean.astype(jnp.bfloat16)
    kmean_lo = (kmean - kmean_hi.astype(jnp.float32)).astype(jnp.bfloat16)
    gate = (_nt_dot(kmean_hi, q) + _nt_dot(kmean_lo, q))[:nrows]
    blk = lax.broadcasted_iota(jnp.int32, (nrows, TQ), 0)
    gate = jnp.where(blk < qi, gate, NEG_INF)
    beaten = jnp.zeros((nrows, TQ), jnp.float32)
    for n in range(nblocks):
        other = gate[n:n + 1, :]
        wins = jnp.where(other > gate, 1.0, jnp.where(other == gate, jnp.where(blk > n, 1.0, 0.0), 0.0))
        beaten = beaten + wins
    past_pen = jnp.where(beaten < MOBA_TOPK, 0.0, NEG_INF)
    own_pen = jnp.where(blk == qi, 0.0, NEG_INF)
    pen_t = jnp.where(blk < qi, past_pen, own_pen)
    pen_t = jnp.concatenate([pen_t, jnp.full((LANES - nrows, TQ), NEG_INF, jnp.float32)], axis=0)
    return pen_t.T


def _moba_kernel(*refs, nblocks):
    nh = HEADS_PER_STEP
    q_refs, k_refs, v_refs = refs[:nh], refs[nh:2 * nh], refs[2 * nh:3 * nh]
    bias_ref, o_ref, m_ref, acc_ref, kmean_ref = refs[3 * nh:]
    qi = pl.program_id(2)

    @pl.when(qi == 0)
    def _():
        kmean_ref[...] = jnp.zeros_like(kmean_ref)
        for hh in range(nh):
            for n in range(nblocks):
                blk = k_refs[hh][n * MOBA_BLOCK:(n + 1) * MOBA_BLOCK, :].astype(jnp.float32)
                kmean_ref[hh, n:n + 1, :] = jnp.mean(blk, axis=0, keepdims=True)

    q_aug = []
    for hh in range(nh):
        _init_state(m_ref.at[hh], acc_ref.at[hh])
        q = q_refs[hh][...]
        pen = _moba_block_penalty(q, kmean_ref[hh], qi, nblocks)
        q_aug.append(jnp.concatenate([q, pen.astype(jnp.bfloat16)], axis=1))

    def logits(k0, ntiles, near):
        rows = _key_rows(k0, ntiles)
        n = ntiles * TK
        block_of_key = k0 + lax.broadcasted_iota(jnp.int32, (n, LANES), 0) // MOBA_BLOCK
        one_hot = jnp.where(lax.broadcasted_iota(jnp.int32, (n, LANES), 1) == block_of_key, 1.0, 0.0)
        one_hot = one_hot.astype(jnp.bfloat16)
        out = []
        for hh in range(nh):
            z = _nt_dot(q_aug[hh], jnp.concatenate([k_refs[hh][rows, :], one_hot], axis=1))
            out.append(z + _near_bias(bias_ref, hh, near) if near else z)
        return tuple(out)

    def consume(k0, ntiles, zs):
        rows = _key_rows(k0, ntiles)
        for hh in range(nh):
            _softmax_tile(zs[hh], _with_ones(v_refs[hh][rows, :]), m_ref.at[hh], acc_ref.at[hh])

    _causal_key_loop(qi, logits, consume)
    for hh in range(nh):
        o_ref[:, hh * HEAD_DIM:(hh + 1) * HEAD_DIM] = _normalized(acc_ref.at[hh], HEAD_DIM).astype(o_ref.dtype)


def _head_specs(block_rows, row_index, col_off):
    return [pl.BlockSpec((block_rows, HEAD_DIM),
                         lambda b, h, i, hh=hh: (row_index(b, i), col_off // HEAD_DIM + h * HEADS_PER_STEP + hh))
            for hh in range(HEADS_PER_STEP)]


def _moba_attention(proj, bias_tiles, batch, seq):
    assert TQ == MOBA_BLOCK and TK == MOBA_BLOCK and seq % MOBA_BLOCK == 0
    assert MOBA_HEADS % HEADS_PER_STEP == 0
    nq = seq // TQ
    nblocks = seq // MOBA_BLOCK
    assert MOBA_TOPK <= nblocks <= LANES
    nh = HEADS_PER_STEP
    w = nh * HEAD_DIM
    q_row = lambda b, i: b * nq + i
    seq_row = lambda b, i: b
    return pl.pallas_call(
        functools.partial(_moba_kernel, nblocks=nblocks),
        grid=(batch, MOBA_HEADS // nh, nq),
        in_specs=(_head_specs(TQ, q_row, OFF_MQ) + _head_specs(seq, seq_row, OFF_MK)
                  + _head_specs(seq, seq_row, OFF_MV)
                  + [pl.BlockSpec((nh, 2, TQ, TK), lambda b, h, i: (h, 0, 0, 0))]),
        out_specs=pl.BlockSpec((TQ, w), lambda b, h, i: (b * nq + i, h)),
        out_shape=jax.ShapeDtypeStruct((batch * seq, MOBA_HEADS * HEAD_DIM), jnp.bfloat16),
        scratch_shapes=[pltpu.VMEM((nh, TQ, LANES), jnp.float32),
                        pltpu.VMEM((nh, TQ, HEAD_DIM + LANES), jnp.float32),
                        pltpu.VMEM((nh, LANES, HEAD_DIM), jnp.float32)],
        compiler_params=_params("parallel", "parallel", "arbitrary"),
        name="moba_attention",
    )(*([proj] * (3 * nh)), bias_tiles)


def _dsa_prep_kernel(x_ref, g_ref, wuk_ref, wuv_ref, k_ref, v_ref, iklo_ref, ikhi_ref, iw_ref):
    ckv = x_ref[:, :KV_LATENT]
    ckv = ckv * lax.rsqrt(jnp.mean(ckv * ckv, axis=-1, keepdims=True) + RMS_EPS) * g_ref[...]
    ckv = ckv.astype(jnp.bfloat16)
    k_ref[...] = jnp.dot(ckv, wuk_ref[...], preferred_element_type=jnp.float32).astype(k_ref.dtype)
    v_ref[...] = jnp.dot(ckv, wuv_ref[...], preferred_element_type=jnp.float32).astype(v_ref.dtype)

    grp = x_ref[:, KV_LATENT:]
    lane = lax.broadcasted_iota(jnp.int32, grp.shape, 1)
    is_key = lane < IDX_DIM
    mu = jnp.sum(jnp.where(is_key, grp, 0.0), axis=-1, keepdims=True) / IDX_DIM
    cen = jnp.where(is_key, grp - mu, 0.0)
    var = jnp.sum(cen * cen, axis=-1, keepdims=True) / IDX_DIM
    key_lo = cen * lax.rsqrt(var + LN_EPS)
    iklo_ref[...] = key_lo.astype(iklo_ref.dtype)
    ikhi_ref[...] = pltpu.roll(key_lo, IDX_DIM, axis=1).astype(ikhi_ref.dtype)
    w = pltpu.roll(grp, LANES - IDX_DIM, axis=1)
    iw_ref[...] = jnp.where(lane < IDX_HEADS, w * (IDX_HEADS ** -0.5) * (IDX_DIM ** -0.5), 0.0)


def _dsa_prep(tail, kv_norm_g, wuk_flat, wuv_flat, rows=512):
    m = tail.shape[0]
    n = DSA_HEADS * HEAD_DIM
    row = lambda w: pl.BlockSpec((rows, w), lambda i: (i, 0))
    full = lambda a: pl.BlockSpec(a.shape, lambda i: (0, 0))
    g = kv_norm_g.reshape(1, KV_LATENT)
    return pl.pallas_call(
        _dsa_prep_kernel,
        grid=(m // rows,),
        in_specs=[row(TAIL_W), full(g), full(wuk_flat), full(wuv_flat)],
        out_specs=[row(n), row(n), row(LANES), row(LANES), row(LANES)],
        out_shape=[jax.ShapeDtypeStruct((m, n), jnp.bfloat16), jax.ShapeDtypeStruct((m, n), jnp.bfloat16),
                   jax.ShapeDtypeStruct((m, LANES), jnp.bfloat16), jax.ShapeDtypeStruct((m, LANES), jnp.bfloat16),
                   jax.ShapeDtypeStruct((m, LANES), jnp.float32)],
        compiler_params=_params("parallel"),
        name="dsa_prep",
    )(tail, g, wuk_flat, wuv_flat)


def _sortable(x):
    b = pltpu.bitcast(x, jnp.int32)
    return b ^ ((b >> 31) & jnp.int32(0x7FFFFFFF))


def _dsa_select_kernel(iq_ref, iklo_ref, ikhi_ref, iwt_ref, o_ref, key_ref, eqidx_ref, cut_ref, *, n_top):
    qi = pl.program_id(1)
    n_ch = qi + 1
    shape = (SEL_CH, SEL_TQ)
    t = qi * SEL_TQ + lax.broadcasted_iota(jnp.int32, shape, 1)
    key_in_chunk = lax.broadcasted_iota(jnp.int32, shape, 0)
    chunk = lambda c: pl.ds(pl.multiple_of(c * SEL_CH, SEL_CH), SEL_CH)

    def score_body(c, carry):
        rows = chunk(c)
        k_lo = iklo_ref[rows, :]
        k_hi = ikhi_ref[rows, :]
        acc = jnp.zeros(shape, jnp.float32)
        for pair in range(IDX_HEADS // 2):
            q2 = iq_ref[:, pair * LANES:(pair + 1) * LANES]
            for half, k_half in enumerate((k_lo, k_hi)):
                hd = 2 * pair + half
                acc = acc + jnp.maximum(_nt_dot(k_half, q2), 0.0) * iwt_ref[hd:hd + 1, :]
        score = jnp.where(c * SEL_CH + key_in_chunk <= t, acc, NEG_INF)
        key_ref[c] = _sortable(score)
        return carry

    lax.fori_loop(0, n_ch, score_body, 0)

    def count(ref, pred):
        def body(c, acc):
            hit = jnp.where(pred(ref[c]), 1.0, 0.0)
            return acc + jnp.sum(hit.reshape(SEL_CH // COUNT_ROWS, COUNT_ROWS, SEL_TQ), axis=0)
        acc = lax.fori_loop(0, n_ch, body, jnp.zeros((COUNT_ROWS, SEL_TQ), jnp.float32))
        return jnp.sum(acc, axis=0, keepdims=True)

    v = jnp.where(count(key_ref, lambda x: x >= 0) >= n_top, jnp.int32(0), jnp.int32(INT_MIN))

    def bit_body(i, v):
        trial = v + jnp.left_shift(jnp.int32(1), 30 - i)
        return jnp.where(count(key_ref, lambda x: x >= trial) >= n_top, trial, v)

    v = lax.fori_loop(0, 31, bit_body, v)

    n_above = count(key_ref, lambda x: x > v)
    need = n_top - n_above
    n_tied = count(key_ref, lambda x: x >= v) - n_above
    far = jnp.int32(2 ** 30)

    def eq_body(c, carry):
        eqidx_ref[c] = jnp.where(key_ref[c] == v, c * SEL_CH + key_in_chunk, far)
        return carry

    lax.fori_loop(0, n_ch, eq_body, 0)
    nbits = max(1, (key_ref.shape[0] * SEL_CH - 1).bit_length())
    cut_ref[...] = jnp.full_like(cut_ref, far - 1)

    @pl.when(jnp.max(jnp.where(n_tied > need, 1.0, 0.0)) > 0.0)
    def _():
        def idx_body(i, cut):
            trial = cut + jnp.left_shift(jnp.int32(1), nbits - 1 - i)
            return jnp.where(count(eqidx_ref, lambda x: x < trial) <= need - 1.0, trial, cut)

        cut_ref[...] = lax.fori_loop(0, nbits, idx_body, jnp.zeros((1, SEL_TQ), jnp.int32))

    cut = cut_ref[...]
    o_ref[...] = jnp.full_like(o_ref, NEG_INF)

    def out_body(c, carry):
        picked = jnp.where(key_ref[c] > v, 0.0, jnp.where(eqidx_ref[c] <= cut, 0.0, NEG_INF))
        o_ref[0, c] = jnp.where(c * SEL_CH + key_in_chunk <= t, picked, NEG_INF).T
        return carry

    lax.fori_loop(0, n_ch, out_body, 0)


def _dsa_select(proj, ik_lo, ik_hi, iw, batch, seq):
    assert SEL_TQ == SEL_CH and seq % SEL_CH == 0 and SEL_CH >= DSA_TOPK
    n_top = min(DSA_TOPK, seq // 4)
    nq = seq // SEL_TQ
    n_ch = seq // SEL_CH
    iq_w = IDX_HEADS * IDX_DIM
    key_spec = pl.BlockSpec((seq, LANES), lambda b, i: (b, 0))
    iw_t = iw[:, :IDX_HEADS].T
    return pl.pallas_call(
        functools.partial(_dsa_select_kernel, n_top=n_top),
        grid=(batch, nq),
        in_specs=[pl.BlockSpec((SEL_TQ, iq_w), lambda b, i: (b * nq + i, OFF_IQ // iq_w)),
                  key_spec, key_spec,
                  pl.BlockSpec((IDX_HEADS, SEL_TQ), lambda b, i: (0, b * nq + i))],
        out_specs=pl.BlockSpec((1, n_ch, SEL_TQ, SEL_CH), lambda b, i: (b, 0, i, 0)),
        out_shape=jax.ShapeDtypeStruct((batch, n_ch, seq, SEL_CH), jnp.float32),
        scratch_shapes=[pltpu.VMEM((n_ch, SEL_CH, SEL_TQ), jnp.int32), pltpu.VMEM((n_ch, SEL_CH, SEL_TQ), jnp.int32),
                        pltpu.VMEM((1, SEL_TQ), jnp.int32)],
        compiler_params=_params("parallel", "arbitrary"),
        name="dsa_select",
    )(proj, ik_lo, ik_hi, iw_t)


def _dsa_attn_kernel(*refs):
    nh = HEADS_PER_STEP
    q_refs = refs[:nh]
    k_ref, v_ref, bias_ref, mask_ref, o_ref, m_ref, acc_ref = refs[nh:]
    qi = pl.program_id(2)
    heads = [slice(hh * HEAD_DIM, (hh + 1) * HEAD_DIM) for hh in range(nh)]
    for hh in range(nh):
        _init_state(m_ref.at[hh], acc_ref.at[hh])

    def logits(k0, ntiles, near):
        rows = _key_rows(k0, ntiles)
        mask = mask_ref[0, k0] if ntiles == 1 else jnp.concatenate(
            [mask_ref[0, k0 + j] for j in range(ntiles)], axis=1)
        out = []
        for hh, cols in enumerate(heads):
            z = _nt_dot(q_refs[hh][...], k_ref[rows, cols]) + mask
            out.append(z + _near_bias(bias_ref, hh, near) if near else z)
        return tuple(out)

    def consume(k0, ntiles, zs):
        rows = _key_rows(k0, ntiles)
        for hh, cols in enumerate(heads):
            _softmax_tile(zs[hh], _with_ones(v_ref[rows, cols]), m_ref.at[hh], acc_ref.at[hh])

    _causal_key_loop(qi, logits, consume)
    for hh, cols in enumerate(heads):
        o_ref[:, cols] = _normalized(acc_ref.at[hh], HEAD_DIM).astype(o_ref.dtype)


def _dsa_attention(proj, k_dsa, v_dsa, mask, bias_tiles, batch, seq):
    assert DSA_HEADS % HEADS_PER_STEP == 0
    nq = seq // TQ
    nh = HEADS_PER_STEP
    w = nh * HEAD_DIM
    kv_spec = pl.BlockSpec((seq, w), lambda b, h, i: (b, h))
    return pl.pallas_call(
        _dsa_attn_kernel,
        grid=(batch, DSA_HEADS // nh, nq),
        in_specs=(_head_specs(TQ, lambda b, i: b * nq + i, OFF_CQ) + [
            kv_spec, kv_spec,
            pl.BlockSpec((nh, 2, TQ, TK), lambda b, h, i: (h, 0, 0, 0)),
            pl.BlockSpec((1, seq // TK, TQ, TK), lambda b, h, i: (b, 0, i, 0)),
        ]),
        out_specs=pl.BlockSpec((TQ, w), lambda b, h, i: (b * nq + i, h)),
        out_shape=jax.ShapeDtypeStruct((batch * seq, DSA_HEADS * HEAD_DIM), jnp.bfloat16),
        scratch_shapes=[pltpu.VMEM((nh, TQ, LANES), jnp.float32),
                        pltpu.VMEM((nh, TQ, HEAD_DIM + LANES), jnp.float32)],
        compiler_params=_params("parallel", "parallel", "arbitrary"),
        name="dsa_attention",
    )(*([proj] * nh), k_dsa, v_dsa, bias_tiles, mask)


def _mixers(proj, tail, bias_tiles, diff_lambda, diff_subln_g, kv_norm_g, w_uk, w_uv,
            layer_idx, batch, seq):
    lambda_init = 0.8 - 0.6 * math.exp(-0.3 * layer_idx)
    c0 = 2 * DIFF_HEADS
    c1 = c0 + MOBA_HEADS
    flat = lambda w: w.transpose(1, 0, 2).reshape(KV_LATENT, DSA_HEADS * HEAD_DIM).astype(jnp.bfloat16)
    y_diff = _diff_attention(proj, bias_tiles, diff_lambda, diff_subln_g, lambda_init, batch, seq)
    y_moba = _moba_attention(proj, bias_tiles[c0:c1], batch, seq)
    k_dsa, v_dsa, ik_lo, ik_hi, iw = _dsa_prep(tail, kv_norm_g, flat(w_uk), flat(w_uv))
    mask = _dsa_select(proj, ik_lo, ik_hi, iw, batch, seq)
    y_dsa = _dsa_attention(proj, k_dsa, v_dsa, mask, bias_tiles[c1:], batch, seq)
    return [y_diff, y_moba, y_dsa]


def _tail_weights(w_in_b):
    pad = jnp.zeros(w_in_b.shape[:2] + (TAIL_W - (D_IN - OFF_IK) - KV_LATENT,), w_in_b.dtype)
    return jnp.concatenate([w_in_b[:, :, OFF_CKV:OFF_IQ], w_in_b[:, :, OFF_IK:], pad], axis=2)


def kernel(x, ln_emb_g, ln_emb_b, rel_bias, w_in, diff_lambda, diff_subln_g, kv_norm_g, w_uk, w_uv, w_o,
           ln1_g, ln1_b, w_up, w_down, ln2_g, ln2_b):
    batch, seq, d_model = x.shape
    depth = w_in.shape[0]
    alpha = (2.0 * depth) ** 0.25
    bf16 = jnp.bfloat16
    bias_tiles = _bias_tiles(rel_bias)
    w_down_b = w_down.astype(bf16)
    w_in_b = w_in.astype(bf16)
    w_tail_b = _tail_weights(w_in_b)
    h, hb = _layer_norm(x.reshape(batch * seq, d_model), None, ln_emb_g, ln_emb_b, 1.0)
    for l in range(depth):
        proj = _matmul_bf16w(hb, w_in_b, l, N_MAIN, bf16)
        tail = _matmul_bf16w(hb, w_tail_b, l, TAIL_W, jnp.float32, tn=TAIL_W)
        mix = _mixers(proj, tail, bias_tiles, diff_lambda[l], diff_subln_g[l], kv_norm_g[l],
                      w_uk[l], w_uv[l], l, batch, seq)
        y = _matmul_f32w(mix, w_o, l, d_model, jnp.float32)
        h, hb = _layer_norm(h, y, ln1_g[l], ln1_b[l], alpha)
        up = _matmul_f32w([hb], w_up, l, w_up.shape[2], bf16, relu2=True)
        y = _matmul_acc(up, w_down_b, l)
        h, hb = _layer_norm(h, y, ln2_g[l], ln2_b[l], alpha)
    return h.reshape(batch, seq, d_model)
```

```python
import functools
import math

import jax
import jax.numpy as jnp
import numpy as np
from jax import lax
from jax.experimental import pallas as pl
from jax.experimental.pallas import tpu as pltpu

HEAD_DIM = 128
DIFF_HEADS = 6
DIFF_DV = 2 * HEAD_DIM
MOBA_HEADS = 10
DSA_HEADS = 10
MOBA_BLOCK = 256
MOBA_TOPK = 3
DSA_TOPK = 256
KV_LATENT = 512
IDX_HEADS = 32
IDX_DIM = 64
NUM_BUCKETS = 32
MAX_DISTANCE = 128
LN_EPS = 1e-5
RMS_EPS = 1e-5
NEG_INF = -1e30

OFF_DQ = 0
OFF_DK = OFF_DQ + 2 * DIFF_HEADS * HEAD_DIM
OFF_DV = OFF_DK + 2 * DIFF_HEADS * HEAD_DIM
OFF_MQ = OFF_DV + DIFF_HEADS * DIFF_DV
OFF_MK = OFF_MQ + MOBA_HEADS * HEAD_DIM
OFF_MV = OFF_MK + MOBA_HEADS * HEAD_DIM
OFF_CQ = OFF_MV + MOBA_HEADS * HEAD_DIM
OFF_CKV = OFF_CQ + DSA_HEADS * HEAD_DIM
OFF_IQ = OFF_CKV + KV_LATENT
OFF_IK = OFF_IQ + IDX_HEADS * IDX_DIM
OFF_IW = OFF_IK + IDX_DIM
D_IN = OFF_IW + IDX_HEADS
N_MAIN = OFF_IK
TAIL_W = KV_LATENT + 128

LANES = 128
VMEM_LIMIT = 56 * 1024 * 1024
TQ = 256
TK = 256
SEL_TQ = TQ
SEL_CH = TK
COUNT_ROWS = 32
INT_MIN = -2 ** 31
FAR_TILES = 4
HEADS_PER_STEP = 5
DIFF_HEADS_PER_STEP = 3
EXP2_SCALE = HEAD_DIM ** -0.5 * math.log2(math.e)


def _t5_thresholds():
    n = np.arange(0, 4 * MAX_DISTANCE)
    max_exact = NUM_BUCKETS // 2
    nf = np.maximum(n, 1).astype(np.float64)
    large = max_exact + (np.log(nf / max_exact) / math.log(MAX_DISTANCE / max_exact)
                         * (NUM_BUCKETS - max_exact)).astype(np.int32)
    bucket = np.where(n < max_exact, n, np.minimum(large, NUM_BUCKETS - 1))
    return [int(np.argmax(bucket >= b)) for b in range(NUM_BUCKETS)]


T5_THRESHOLDS = _t5_thresholds()
assert T5_THRESHOLDS[-1] <= TK // 2


def _params(*sem):
    return pltpu.CompilerParams(dimension_semantics=sem, vmem_limit_bytes=VMEM_LIMIT)


def _ln_kernel(x_ref, g_ref, b_ref, o_ref, ob_ref):
    z = x_ref[...]
    mu = jnp.mean(z, axis=-1, keepdims=True)
    zc = z - mu
    var = jnp.mean(zc * zc, axis=-1, keepdims=True)
    out = zc * lax.rsqrt(var + LN_EPS) * g_ref[...] + b_ref[...]
    o_ref[...] = out
    ob_ref[...] = out.astype(jnp.bfloat16)


def _layer_norm(x, g, b, rows=256):
    m, d = x.shape
    row_spec = pl.BlockSpec((rows, d), lambda i: (i, 0))
    vec_spec = pl.BlockSpec((1, d), lambda i: (0, 0))
    return pl.pallas_call(
        _ln_kernel,
        grid=(m // rows,),
        in_specs=[row_spec, vec_spec, vec_spec],
        out_specs=[row_spec, row_spec],
        out_shape=[jax.ShapeDtypeStruct((m, d), jnp.float32), jax.ShapeDtypeStruct((m, d), jnp.bfloat16)],
        compiler_params=_params("parallel"),
        name="layer_norm",
    )(x, g.reshape(1, d), b.reshape(1, d))


def _mm_kernel(*refs, relu2, res_scale):
    if res_scale is None:
        *a_refs, w_ref, o_ref, wb_ref = refs
    else:
        *a_refs, w_ref, res_ref, o_ref, wb_ref = refs

    @pl.when(pl.program_id(1) == 0)
    def _():
        wb_ref[...] = w_ref[...].astype(jnp.bfloat16)

    acc, row = None, 0
    for a_ref in a_refs:
        part = jnp.dot(a_ref[...], wb_ref[row:row + a_ref.shape[1], :], preferred_element_type=jnp.float32)
        acc = part if acc is None else acc + part
        row += a_ref.shape[1]
    if relu2:
        acc = jnp.maximum(acc, 0.0)
        acc = acc * acc
    if res_scale is not None:
        acc = res_scale * res_ref[...] + acc
    o_ref[...] = acc.astype(o_ref.dtype)


def _mm_acc_kernel(a_ref, w_ref, res_ref, o_ref, *, res_scale):
    @pl.when(pl.program_id(2) == 0)
    def _():
        o_ref[...] = res_scale * res_ref[...]

    o_ref[...] += jnp.dot(a_ref[...], w_ref[...], preferred_element_type=jnp.float32)


def _matmul_f32w(a_parts, w, layer, n, out_dtype, relu2=False, residual=None, res_scale=None, tm=1024, tn=512):
    m = a_parts[0].shape[0]
    k = sum(a.shape[1] for a in a_parts)
    tn = min(tn, n)
    assert n % tn == 0 and m % tm == 0 and k == w.shape[1]
    res = [] if residual is None else [residual]
    return pl.pallas_call(
        functools.partial(_mm_kernel, relu2=relu2, res_scale=res_scale),
        grid=(n // tn, m // tm),
        in_specs=([pl.BlockSpec((tm, a.shape[1]), lambda j, i: (i, 0)) for a in a_parts]
                  + [pl.BlockSpec((None, k, tn), lambda j, i: (layer, 0, j))]
                  + [pl.BlockSpec((tm, tn), lambda j, i: (i, j)) for _ in res]),
        out_specs=pl.BlockSpec((tm, tn), lambda j, i: (i, j)),
        out_shape=jax.ShapeDtypeStruct((m, n), out_dtype),
        scratch_shapes=[pltpu.VMEM((k, tn), jnp.bfloat16)],
        compiler_params=_params("parallel", "arbitrary"),
        name="matmul",
    )(*a_parts, w, *res)


def _mm_bf16_kernel(a_ref, w_ref, o_ref):
    o_ref[...] = jnp.dot(a_ref[...], w_ref[...], preferred_element_type=jnp.float32).astype(o_ref.dtype)


def _matmul_bf16w(a, w, layer, n, out_dtype, tm=1024, tn=1024):
    m, k = a.shape
    return pl.pallas_call(
        _mm_bf16_kernel,
        grid=(m // tm, n // tn),
        in_specs=[pl.BlockSpec((tm, k), lambda i, j: (i, 0)),
                  pl.BlockSpec((None, k, tn), lambda i, j: (layer, 0, j))],
        out_specs=pl.BlockSpec((tm, tn), lambda i, j: (i, j)),
        out_shape=jax.ShapeDtypeStruct((m, n), out_dtype),
        compiler_params=_params("parallel", "parallel"),
        name="matmul_bf16w",
    )(a, w)


def _matmul_acc(a, w, layer, residual, res_scale, tm=1024, tn=1024, tk=4096):
    m, k = a.shape
    n = w.shape[2]
    return pl.pallas_call(
        functools.partial(_mm_acc_kernel, res_scale=res_scale),
        grid=(m // tm, n // tn, k // tk),
        in_specs=[pl.BlockSpec((tm, tk), lambda i, j, l: (i, l)),
                  pl.BlockSpec((None, tk, tn), lambda i, j, l: (layer, l, j)),
                  pl.BlockSpec((tm, tn), lambda i, j, l: (i, j))],
        out_specs=pl.BlockSpec((tm, tn), lambda i, j, l: (i, j)),
        out_shape=jax.ShapeDtypeStruct((m, n), jnp.float32),
        compiler_params=_params("parallel", "parallel", "arbitrary"),
        name="matmul_acc",
    )(a, w, residual)


def _bias_tiles_kernel(tab_ref, o_ref):
    col = pl.program_id(0)
    row = lax.broadcasted_iota(jnp.int32, (TQ, TK), 0)
    key = lax.broadcasted_iota(jnp.int32, (TQ, TK), 1)
    last = tab_ref[NUM_BUCKETS - 1, col]
    for tile, shift in ((0, 0), (1, TK)):
        dist = row - key + shift
        bias = jnp.full((TQ, TK), tab_ref[0, col], jnp.float32)
        for b in range(1, NUM_BUCKETS):
            bias = jnp.where(dist >= T5_THRESHOLDS[b], tab_ref[b, col], bias)
        o_ref[0, tile] = jnp.where(dist >= 0, (bias - last) * (HEAD_DIM ** 0.5), NEG_INF)


def _bias_tiles(rel_bias):
    ncols = rel_bias.shape[1]
    return pl.pallas_call(
        _bias_tiles_kernel,
        grid=(ncols,),
        in_specs=[pl.BlockSpec(memory_space=pltpu.SMEM)],
        out_specs=pl.BlockSpec((1, 2, TQ, TK), lambda c: (c, 0, 0, 0)),
        out_shape=jax.ShapeDtypeStruct((ncols, 2, TQ, TK), jnp.float32),
        compiler_params=_params("parallel"),
        name="t5_bias_tiles",
    )(rel_bias)


def _nt_dot(a, b):
    return lax.dot_general(a, b, (((1,), (1,)), ((), ())), preferred_element_type=jnp.float32)


def _lanes(x, n):
    return x if n == LANES else jnp.concatenate([x] * (n // LANES), axis=1)


def _with_ones(v):
    return jnp.concatenate([v, jnp.ones((v.shape[0], LANES), v.dtype)], axis=1)


def _softmax_tile(z, v, m_ref, acc_ref, l_ref=None):
    m_old = m_ref[...]
    m_new = jnp.maximum(m_old, jnp.max(z, axis=-1, keepdims=True))
    alpha = jnp.exp2((m_old - m_new) * EXP2_SCALE)
    p = jnp.exp2((z - _lanes(m_new, z.shape[1])) * EXP2_SCALE)
    if l_ref is not None:
        l_ref[...] = alpha * l_ref[...] + jnp.sum(p, axis=-1, keepdims=True)
    acc_ref[...] = _lanes(alpha, acc_ref.shape[1]) * acc_ref[...] + jnp.dot(
        p.astype(jnp.bfloat16), v, preferred_element_type=jnp.float32)
    m_ref[...] = m_new


def _init_state(m_ref, acc_ref):
    m_ref[...] = jnp.full_like(m_ref, NEG_INF)
    acc_ref[...] = jnp.zeros_like(acc_ref)


def _normalized(acc_ref, dv):
    acc = acc_ref[...]
    return acc[:, :dv] / _lanes(acc[:, dv:], dv)


def _key_rows(k0, ntiles):
    return pl.ds(pl.multiple_of(k0 * TK, TK), ntiles * TK)


def _near_bias(bias_ref, col, near):
    return bias_ref[col, 0] if near == 1 else jnp.concatenate([bias_ref[col, 1], bias_ref[col, 0]], axis=1)


def _causal_key_loop(qi, logits_fn, consume_fn):
    n_far = jnp.maximum(qi - 1, 0)
    n_big = n_far // FAR_TILES
    rem = n_far - n_big * FAR_TILES

    def tile_fn(k0, ntiles, near):
        consume_fn(k0, ntiles, logits_fn(k0, ntiles, near))

    def big_body(kb, carry):
        tile_fn(kb * FAR_TILES, FAR_TILES, 0)
        return carry

    lax.fori_loop(0, n_big, big_body, 0)

    @pl.when(rem >= 2)
    def _():
        tile_fn(n_big * FAR_TILES, 2, 0)

    @pl.when(rem % 2 == 1)
    def _():
        tile_fn(n_far - 1, 1, 0)

    @pl.when(qi >= 1)
    def _():
        tile_fn(qi - 1, 2, 2)

    @pl.when(qi == 0)
    def _():
        tile_fn(qi, 1, 1)


def _diff_kernel(q_ref, k_ref, v_ref, bias_ref, lam_ref, g_ref, o_ref, m_ref, l_ref, acc_ref, *, lambda_init):
    qi = pl.program_id(2)
    nmaps = 2 * DIFF_HEADS_PER_STEP
    for c in range(nmaps):
        _init_state(m_ref.at[c], acc_ref.at[c])
        l_ref[c] = jnp.zeros_like(l_ref[c])

    def logits(k0, ntiles, near):
        rows = _key_rows(k0, ntiles)
        out = []
        for c in range(nmaps):
            cols = slice(c * HEAD_DIM, (c + 1) * HEAD_DIM)
            z = _nt_dot(q_ref[:, cols], k_ref[rows, cols])
            out.append(z + _near_bias(bias_ref, c, near) if near else z)
        return tuple(out)

    def consume(k0, ntiles, zs):
        rows = _key_rows(k0, ntiles)
        for c in range(nmaps):
            hh = c // 2
            _softmax_tile(zs[c], v_ref[rows, hh * DIFF_DV:(hh + 1) * DIFF_DV], m_ref.at[c], acc_ref.at[c],
                          l_ref.at[c])

    _causal_key_loop(qi, logits, consume)

    lv = lam_ref[...]
    lam = (jnp.exp(jnp.sum(lv[0:1] * lv[1:2], axis=-1, keepdims=True))
           - jnp.exp(jnp.sum(lv[2:3] * lv[3:4], axis=-1, keepdims=True)) + lambda_init)
    for hh in range(DIFF_HEADS_PER_STEP):
        a1 = acc_ref[2 * hh] / _lanes(l_ref[2 * hh], DIFF_DV)
        a2 = acc_ref[2 * hh + 1] / _lanes(l_ref[2 * hh + 1], DIFF_DV)
        o = a1 - lam * a2
        o = o * lax.rsqrt(jnp.mean(o * o, axis=-1, keepdims=True) + RMS_EPS) * g_ref[...]
        o_ref[:, hh * DIFF_DV:(hh + 1) * DIFF_DV] = (o * (1.0 - lambda_init)).astype(o_ref.dtype)


def _diff_attention(proj, bias_tiles, lam_vecs, subln_g, lambda_init, batch, seq):
    assert DIFF_HEADS % DIFF_HEADS_PER_STEP == 0
    nq = seq // TQ
    qk_w = 2 * DIFF_HEADS_PER_STEP * HEAD_DIM
    v_w = DIFF_HEADS_PER_STEP * DIFF_DV
    return pl.pallas_call(
        functools.partial(_diff_kernel, lambda_init=lambda_init),
        grid=(batch, DIFF_HEADS // DIFF_HEADS_PER_STEP, nq),
        in_specs=[
            pl.BlockSpec((TQ, qk_w), lambda b, h, i: (b * nq + i, OFF_DQ // qk_w + h)),
            pl.BlockSpec((seq, qk_w), lambda b, h, i: (b, OFF_DK // qk_w + h)),
            pl.BlockSpec((seq, v_w), lambda b, h, i: (b, OFF_DV // v_w + h)),
            pl.BlockSpec((2 * DIFF_HEADS_PER_STEP, 2, TQ, TK), lambda b, h, i: (h, 0, 0, 0)),
            pl.BlockSpec((4, HEAD_DIM), lambda b, h, i: (0, 0)),
            pl.BlockSpec((1, DIFF_DV), lambda b, h, i: (0, 0)),
        ],
        out_specs=pl.BlockSpec((TQ, v_w), lambda b, h, i: (b * nq + i, h)),
        out_shape=jax.ShapeDtypeStruct((batch * seq, DIFF_HEADS * DIFF_DV), jnp.bfloat16),
        scratch_shapes=[pltpu.VMEM((2 * DIFF_HEADS_PER_STEP, TQ, LANES), jnp.float32),
                        pltpu.VMEM((2 * DIFF_HEADS_PER_STEP, TQ, LANES), jnp.float32),
                        pltpu.VMEM((2 * DIFF_HEADS_PER_STEP, TQ, DIFF_DV), jnp.float32)],
        compiler_params=_params("parallel", "parallel", "arbitrary"),
        name="diff_attention",
    )(proj, proj, proj, bias_tiles, lam_vecs, subln_g.reshape(1, DIFF_DV))


def _moba_block_penalty(q, kmean, qi, nblocks):
    nrows = -(-nblocks // 8) * 8
    kmean_hi = kmean.astype(jnp.bfloat16)
    kmean_lo = (kmean - kmean_hi.astype(jnp.float32)).astype(jnp.bfloat16)
    gate = (_nt_dot(kmean_hi, q) + _nt_dot(kmean_lo, q))[:nrows]
    blk = lax.broadcasted_iota(jnp.int32, (nrows, TQ), 0)
    gate = jnp.where(blk < qi, gate, NEG_INF)
    beaten = jnp.zeros((nrows, TQ), jnp.float32)
    for n in range(nblocks):
        other = gate[n:n + 1, :]
        wins = jnp.where(other > gate, 1.0, jnp.where(other == gate, jnp.where(blk > n, 1.0, 0.0), 0.0))
        beaten = beaten + wins
    past_pen = jnp.where(beaten < MOBA_TOPK, 0.0, NEG_INF)
    own_pen = jnp.where(blk == qi, 0.0, NEG_INF)
    pen_t = jnp.where(blk < qi, past_pen, own_pen)
    pen_t = jnp.concatenate([pen_t, jnp.full((LANES - nrows, TQ), NEG_INF, jnp.float32)], axis=0)
    return pen_t.T


def _moba_kernel(*refs, nblocks):
    nh = HEADS_PER_STEP
    q_refs, k_refs, v_refs = refs[:nh], refs[nh:2 * nh], refs[2 * nh:3 * nh]
    bias_ref, o_ref, m_ref, acc_ref, kmean_ref = refs[3 * nh:]
    qi = pl.program_id(2)

    @pl.when(qi == 0)
    def _():
        kmean_ref[...] = jnp.zeros_like(kmean_ref)
        for hh in range(nh):
            for n in range(nblocks):
                blk = k_refs[hh][n * MOBA_BLOCK:(n + 1) * MOBA_BLOCK, :].astype(jnp.float32)
                kmean_ref[hh, n:n + 1, :] = jnp.mean(blk, axis=0, keepdims=True)

    q_aug = []
    for hh in range(nh):
        _init_state(m_ref.at[hh], acc_ref.at[hh])
        q = q_refs[hh][...]
        pen = _moba_block_penalty(q, kmean_ref[hh], qi, nblocks)
        q_aug.append(jnp.concatenate([q, pen.astype(jnp.bfloat16)], axis=1))

    def logits(k0, ntiles, near):
        rows = _key_rows(k0, ntiles)
        n = ntiles * TK
        block_of_key = k0 + lax.broadcasted_iota(jnp.int32, (n, LANES), 0) // MOBA_BLOCK
        one_hot = jnp.where(lax.broadcasted_iota(jnp.int32, (n, LANES), 1) == block_of_key, 1.0, 0.0)
        one_hot = one_hot.astype(jnp.bfloat16)
        out = []
        for hh in range(nh):
            z = _nt_dot(q_aug[hh], jnp.concatenate([k_refs[hh][rows, :], one_hot], axis=1))
            out.append(z + _near_bias(bias_ref, hh, near) if near else z)
        return tuple(out)

    def consume(k0, ntiles, zs):
        rows = _key_rows(k0, ntiles)
        for hh in range(nh):
            _softmax_tile(zs[hh], _with_ones(v_refs[hh][rows, :]), m_ref.at[hh], acc_ref.at[hh])

    _causal_key_loop(qi, logits, consume)
    for hh in range(nh):
        o_ref[:, hh * HEAD_DIM:(hh + 1) * HEAD_DIM] = _normalized(acc_ref.at[hh], HEAD_DIM).astype(o_ref.dtype)


def _head_specs(block_rows, row_index, col_off):
    return [pl.BlockSpec((block_rows, HEAD_DIM),
                         lambda b, h, i, hh=hh: (row_index(b, i), col_off // HEAD_DIM + h * HEADS_PER_STEP + hh))
            for hh in range(HEADS_PER_STEP)]


def _moba_attention(proj, bias_tiles, batch, seq):
    assert TQ == MOBA_BLOCK and TK == MOBA_BLOCK and seq % MOBA_BLOCK == 0
    assert MOBA_HEADS % HEADS_PER_STEP == 0
    nq = seq // TQ
    nblocks = seq // MOBA_BLOCK
    assert MOBA_TOPK <= nblocks <= LANES
    nh = HEADS_PER_STEP
    w = nh * HEAD_DIM
    q_row = lambda b, i: b * nq + i
    seq_row = lambda b, i: b
    return pl.pallas_call(
        functools.partial(_moba_kernel, nblocks=nblocks),
        grid=(batch, MOBA_HEADS // nh, nq),
        in_specs=(_head_specs(TQ, q_row, OFF_MQ) + _head_specs(seq, seq_row, OFF_MK)
                  + _head_specs(seq, seq_row, OFF_MV)
                  + [pl.BlockSpec((nh, 2, TQ, TK), lambda b, h, i: (h, 0, 0, 0))]),
        out_specs=pl.BlockSpec((TQ, w), lambda b, h, i: (b * nq + i, h)),
        out_shape=jax.ShapeDtypeStruct((batch * seq, MOBA_HEADS * HEAD_DIM), jnp.bfloat16),
        scratch_shapes=[pltpu.VMEM((nh, TQ, LANES), jnp.float32),
                        pltpu.VMEM((nh, TQ, HEAD_DIM + LANES), jnp.float32),
                        pltpu.VMEM((nh, LANES, HEAD_DIM), jnp.float32)],
        compiler_params=_params("parallel", "parallel", "arbitrary"),
        name="moba_attention",
    )(*([proj] * (3 * nh)), bias_tiles)


def _dsa_prep_kernel(x_ref, g_ref, wuk_ref, wuv_ref, k_ref, v_ref, iklo_ref, ikhi_ref, iw_ref):
    ckv = x_ref[:, :KV_LATENT]
    ckv = ckv * lax.rsqrt(jnp.mean(ckv * ckv, axis=-1, keepdims=True) + RMS_EPS) * g_ref[...]
    ckv = ckv.astype(jnp.bfloat16)
    k_ref[...] = jnp.dot(ckv, wuk_ref[...], preferred_element_type=jnp.float32).astype(k_ref.dtype)
    v_ref[...] = jnp.dot(ckv, wuv_ref[...], preferred_element_type=jnp.float32).astype(v_ref.dtype)

    grp = x_ref[:, KV_LATENT:]
    lane = lax.broadcasted_iota(jnp.int32, grp.shape, 1)
    is_key = lane < IDX_DIM
    mu = jnp.sum(jnp.where(is_key, grp, 0.0), axis=-1, keepdims=True) / IDX_DIM
    cen = jnp.where(is_key, grp - mu, 0.0)
    var = jnp.sum(cen * cen, axis=-1, keepdims=True) / IDX_DIM
    key_lo = cen * lax.rsqrt(var + LN_EPS)
    iklo_ref[...] = key_lo.astype(iklo_ref.dtype)
    ikhi_ref[...] = pltpu.roll(key_lo, IDX_DIM, axis=1).astype(ikhi_ref.dtype)
    w = pltpu.roll(grp, LANES - IDX_DIM, axis=1)
    iw_ref[...] = jnp.where(lane < IDX_HEADS, w * (IDX_HEADS ** -0.5) * (IDX_DIM ** -0.5), 0.0)


def _dsa_prep(tail, kv_norm_g, wuk_flat, wuv_flat, rows=512):
    m = tail.shape[0]
    n = DSA_HEADS * HEAD_DIM
    row = lambda w: pl.BlockSpec((rows, w), lambda i: (i, 0))
    full = lambda a: pl.BlockSpec(a.shape, lambda i: (0, 0))
    g = kv_norm_g.reshape(1, KV_LATENT)
    return pl.pallas_call(
        _dsa_prep_kernel,
        grid=(m // rows,),
        in_specs=[row(TAIL_W), full(g), full(wuk_flat), full(wuv_flat)],
        out_specs=[row(n), row(n), row(LANES), row(LANES), row(LANES)],
        out_shape=[jax.ShapeDtypeStruct((m, n), jnp.bfloat16), jax.ShapeDtypeStruct((m, n), jnp.bfloat16),
                   jax.ShapeDtypeStruct((m, LANES), jnp.bfloat16), jax.ShapeDtypeStruct((m, LANES), jnp.bfloat16),
                   jax.ShapeDtypeStruct((m, LANES), jnp.float32)],
        compiler_params=_params("parallel"),
        name="dsa_prep",
    )(tail, g, wuk_flat, wuv_flat)


def _sortable(x):
    b = pltpu.bitcast(x, jnp.int32)
    return b ^ ((b >> 31) & jnp.int32(0x7FFFFFFF))


def _dsa_select_kernel(iq_ref, iklo_ref, ikhi_ref, iwt_ref, o_ref, key_ref, eqidx_ref, cut_ref, *, n_top):
    qi = pl.program_id(1)
    n_ch = qi + 1
    shape = (SEL_CH, SEL_TQ)
    t = qi * SEL_TQ + lax.broadcasted_iota(jnp.int32, shape, 1)
    key_in_chunk = lax.broadcasted_iota(jnp.int32, shape, 0)
    chunk = lambda c: pl.ds(pl.multiple_of(c * SEL_CH, SEL_CH), SEL_CH)

    def score_body(c, carry):
        rows = chunk(c)
        k_lo = iklo_ref[rows, :]
        k_hi = ikhi_ref[rows, :]
        acc = jnp.zeros(shape, jnp.float32)
        for pair in range(IDX_HEADS // 2):
            q2 = iq_ref[:, pair * LANES:(pair + 1) * LANES]
            for half, k_half in enumerate((k_lo, k_hi)):
                hd = 2 * pair + half
                acc = acc + jnp.maximum(_nt_dot(k_half, q2), 0.0) * iwt_ref[hd:hd + 1, :]
        score = jnp.where(c * SEL_CH + key_in_chunk <= t, acc, NEG_INF)
        key_ref[c] = _sortable(score)
        return carry

    lax.fori_loop(0, n_ch, score_body, 0)

    def count(ref, pred):
        def body(c, acc):
            hit = jnp.where(pred(ref[c]), 1.0, 0.0)
            return acc + jnp.sum(hit.reshape(SEL_CH // COUNT_ROWS, COUNT_ROWS, SEL_TQ), axis=0)
        acc = lax.fori_loop(0, n_ch, body, jnp.zeros((COUNT_ROWS, SEL_TQ), jnp.float32))
        return jnp.sum(acc, axis=0, keepdims=True)

    v = jnp.where(count(key_ref, lambda x: x >= 0) >= n_top, jnp.int32(0), jnp.int32(INT_MIN))

    def bit_body(i, v):
        trial = v + jnp.left_shift(jnp.int32(1), 30 - i)
        return jnp.where(count(key_ref, lambda x: x >= trial) >= n_top, trial, v)

    v = lax.fori_loop(0, 31, bit_body, v)

    n_above = count(key_ref, lambda x: x > v)
    need = n_top - n_above
    n_tied = count(key_ref, lambda x: x >= v) - n_above
    far = jnp.int32(2 ** 30)

    def eq_body(c, carry):
        eqidx_ref[c] = jnp.where(key_ref[c] == v, c * SEL_CH + key_in_chunk, far)
        return carry

    lax.fori_loop(0, n_ch, eq_body, 0)
    nbits = max(1, (key_ref.shape[0] * SEL_CH - 1).bit_length())
    cut_ref[...] = jnp.full_like(cut_ref, far - 1)

    @pl.when(jnp.max(jnp.where(n_tied > need, 1.0, 0.0)) > 0.0)
    def _():
        def idx_body(i, cut):
            trial = cut + jnp.left_shift(jnp.int32(1), nbits - 1 - i)
            return jnp.where(count(eqidx_ref, lambda x: x < trial) <= need - 1.0, trial, cut)

        cut_ref[...] = lax.fori_loop(0, nbits, idx_body, jnp.zeros((1, SEL_TQ), jnp.int32))

    cut = cut_ref[...]
    o_ref[...] = jnp.full_like(o_ref, NEG_INF)

    def out_body(c, carry):
        picked = jnp.where(key_ref[c] > v, 0.0, jnp.where(eqidx_ref[c] <= cut, 0.0, NEG_INF))
        o_ref[0, c] = jnp.where(c * SEL_CH + key_in_chunk <= t, picked, NEG_INF).T
        return carry

    lax.fori_loop(0, n_ch, out_body, 0)


def _dsa_select(proj, ik_lo, ik_hi, iw, batch, seq):
    assert SEL_TQ == SEL_CH and seq % SEL_CH == 0 and SEL_CH >= DSA_TOPK
    n_top = min(DSA_TOPK, seq // 4)
    nq = seq // SEL_TQ
    n_ch = seq // SEL_CH
    iq_w = IDX_HEADS * IDX_DIM
    key_spec = pl.BlockSpec((seq, LANES), lambda b, i: (b, 0))
    iw_t = iw[:, :IDX_HEADS].T
    return pl.pallas_call(
        functools.partial(_dsa_select_kernel, n_top=n_top),
        grid=(batch, nq),
        in_specs=[pl.BlockSpec((SEL_TQ, iq_w), lambda b, i: (b * nq + i, OFF_IQ // iq_w)),
                  key_spec, key_spec,
                  pl.BlockSpec((IDX_HEADS, SEL_TQ), lambda b, i: (0, b * nq + i))],
        out_specs=pl.BlockSpec((1, n_ch, SEL_TQ, SEL_CH), lambda b, i: (b, 0, i, 0)),
        out_shape=jax.ShapeDtypeStruct((batch, n_ch, seq, SEL_CH), jnp.float32),
        scratch_shapes=[pltpu.VMEM((n_ch, SEL_CH, SEL_TQ), jnp.int32), pltpu.VMEM((n_ch, SEL_CH, SEL_TQ), jnp.int32),
                        pltpu.VMEM((1, SEL_TQ), jnp.int32)],
        compiler_params=_params("parallel", "arbitrary"),
        name="dsa_select",
    )(proj, ik_lo, ik_hi, iw_t)


def _dsa_attn_kernel(*refs):
    nh = HEADS_PER_STEP
    q_refs = refs[:nh]
    k_ref, v_ref, bias_ref, mask_ref, o_ref, m_ref, acc_ref = refs[nh:]
    qi = pl.program_id(2)
    heads = [slice(hh * HEAD_DIM, (hh + 1) * HEAD_DIM) for hh in range(nh)]
    for hh in range(nh):
        _init_state(m_ref.at[hh], acc_ref.at[hh])

    def logits(k0, ntiles, near):
        rows = _key_rows(k0, ntiles)
        mask = mask_ref[0, k0] if ntiles == 1 else jnp.concatenate(
            [mask_ref[0, k0 + j] for j in range(ntiles)], axis=1)
        out = []
        for hh, cols in enumerate(heads):
            z = _nt_dot(q_refs[hh][...], k_ref[rows, cols]) + mask
            out.append(z + _near_bias(bias_ref, hh, near) if near else z)
        return tuple(out)

    def consume(k0, ntiles, zs):
        rows = _key_rows(k0, ntiles)
        for hh, cols in enumerate(heads):
            _softmax_tile(zs[hh], _with_ones(v_ref[rows, cols]), m_ref.at[hh], acc_ref.at[hh])

    _causal_key_loop(qi, logits, consume)
    for hh, cols in enumerate(heads):
        o_ref[:, cols] = _normalized(acc_ref.at[hh], HEAD_DIM).astype(o_ref.dtype)


def _dsa_attention(proj, k_dsa, v_dsa, mask, bias_tiles, batch, seq):
    assert DSA_HEADS % HEADS_PER_STEP == 0
    nq = seq // TQ
    nh = HEADS_PER_STEP
    w = nh * HEAD_DIM
    kv_spec = pl.BlockSpec((seq, w), lambda b, h, i: (b, h))
    return pl.pallas_call(
        _dsa_attn_kernel,
        grid=(batch, DSA_HEADS // nh, nq),
        in_specs=(_head_specs(TQ, lambda b, i: b * nq + i, OFF_CQ) + [
            kv_spec, kv_spec,
            pl.BlockSpec((nh, 2, TQ, TK), lambda b, h, i: (h, 0, 0, 0)),
            pl.BlockSpec((1, seq // TK, TQ, TK), lambda b, h, i: (b, 0, i, 0)),
        ]),
        out_specs=pl.BlockSpec((TQ, w), lambda b, h, i: (b * nq + i, h)),
        out_shape=jax.ShapeDtypeStruct((batch * seq, DSA_HEADS * HEAD_DIM), jnp.bfloat16),
        scratch_shapes=[pltpu.VMEM((nh, TQ, LANES), jnp.float32),
                        pltpu.VMEM((nh, TQ, HEAD_DIM + LANES), jnp.float32)],
        compiler_params=_params("parallel", "parallel", "arbitrary"),
        name="dsa_attention",
    )(*([proj] * nh), k_dsa, v_dsa, bias_tiles, mask)


def _mixers(proj, tail, bias_tiles, diff_lambda, diff_subln_g, kv_norm_g, w_uk, w_uv,
            layer_idx, batch, seq):
    lambda_init = 0.8 - 0.6 * math.exp(-0.3 * layer_idx)
    c0 = 2 * DIFF_HEADS
    c1 = c0 + MOBA_HEADS
    flat = lambda w: w.transpose(1, 0, 2).reshape(KV_LATENT, DSA_HEADS * HEAD_DIM).astype(jnp.bfloat16)
    y_diff = _diff_attention(proj, bias_tiles, diff_lambda, diff_subln_g, lambda_init, batch, seq)
    y_moba = _moba_attention(proj, bias_tiles[c0:c1], batch, seq)
    k_dsa, v_dsa, ik_lo, ik_hi, iw = _dsa_prep(tail, kv_norm_g, flat(w_uk), flat(w_uv))
    mask = _dsa_select(proj, ik_lo, ik_hi, iw, batch, seq)
    y_dsa = _dsa_attention(proj, k_dsa, v_dsa, mask, bias_tiles[c1:], batch, seq)
    return [y_diff, y_moba, y_dsa]


def _tail_weights(w_in_b):
    pad = jnp.zeros(w_in_b.shape[:2] + (TAIL_W - (D_IN - OFF_IK) - KV_LATENT,), w_in_b.dtype)
    return jnp.concatenate([w_in_b[:, :, OFF_CKV:OFF_IQ], w_in_b[:, :, OFF_IK:], pad], axis=2)


def kernel(x, ln_emb_g, ln_emb_b, rel_bias, w_in, diff_lambda, diff_subln_g, kv_norm_g, w_uk, w_uv, w_o,
           ln1_g, ln1_b, w_up, w_down, ln2_g, ln2_b):
    batch, seq, d_model = x.shape
    depth = w_in.shape[0]
    alpha = (2.0 * depth) ** 0.25
    bf16 = jnp.bfloat16
    bias_tiles = _bias_tiles(rel_bias)
    w_down_b = w_down.astype(bf16)
    w_in_b = w_in.astype(bf16)
    w_tail_b = _tail_weights(w_in_b)
    h, hb = _layer_norm(x.reshape(batch * seq, d_model), ln_emb_g, ln_emb_b)
    for l in range(depth):
        proj = _matmul_bf16w(hb, w_in_b, l, N_MAIN, bf16)
        tail = _matmul_bf16w(hb, w_tail_b, l, TAIL_W, jnp.float32, tn=TAIL_W)
        mix = _mixers(proj, tail, bias_tiles, diff_lambda[l], diff_subln_g[l], kv_norm_g[l],
                      w_uk[l], w_uv[l], l, batch, seq)
        z = _matmul_f32w(mix, w_o, l, d_model, jnp.float32, residual=h, res_scale=alpha)
        h, hb = _layer_norm(z, ln1_g[l], ln1_b[l])
        up = _matmul_f32w([hb], w_up, l, w_up.shape[2], bf16, relu2=True)
        z = _matmul_acc(up, w_down_b, l, h, alpha)
        h, hb = _layer_norm(z, ln2_g[l], ln2_b[l])
    return h.reshape(batch, seq, d_model)
```

```python
import functools
import math

import jax
import jax.numpy as jnp
import numpy as np
from jax import lax
from jax.experimental import pallas as pl
from jax.experimental.pallas import tpu as pltpu

HEAD_DIM = 128
DIFF_HEADS = 6
DIFF_DV = 2 * HEAD_DIM
MOBA_HEADS = 10
DSA_HEADS = 10
MOBA_BLOCK = 256
MOBA_TOPK = 3
DSA_TOPK = 256
KV_LATENT = 512
IDX_HEADS = 32
IDX_DIM = 64
NUM_BUCKETS = 32
MAX_DISTANCE = 128
LN_EPS = 1e-5
RMS_EPS = 1e-5
NEG_INF = -1e30

OFF_DQ = 0
OFF_DK = OFF_DQ + 2 * DIFF_HEADS * HEAD_DIM
OFF_DV = OFF_DK + 2 * DIFF_HEADS * HEAD_DIM
OFF_MQ = OFF_DV + DIFF_HEADS * DIFF_DV
OFF_MK = OFF_MQ + MOBA_HEADS * HEAD_DIM
OFF_MV = OFF_MK + MOBA_HEADS * HEAD_DIM
OFF_CQ = OFF_MV + MOBA_HEADS * HEAD_DIM
OFF_CKV = OFF_CQ + DSA_HEADS * HEAD_DIM
OFF_IQ = OFF_CKV + KV_LATENT
OFF_IK = OFF_IQ + IDX_HEADS * IDX_DIM
OFF_IW = OFF_IK + IDX_DIM
D_IN = OFF_IW + IDX_HEADS
N_MAIN = OFF_IK
TAIL_W = KV_LATENT + 128

LANES = 128
VMEM_LIMIT = 56 * 1024 * 1024
TQ = 256
TK = 256
SEL_TQ = TQ
SEL_CH = TK
COUNT_ROWS = 32
INT_MIN = -2 ** 31
FAR_TILES = 4
HEADS_PER_STEP = 5
DIFF_HEADS_PER_STEP = 3
EXP2_SCALE = HEAD_DIM ** -0.5 * math.log2(math.e)


def _t5_thresholds():
    n = np.arange(0, 4 * MAX_DISTANCE)
    max_exact = NUM_BUCKETS // 2
    nf = np.maximum(n, 1).astype(np.float64)
    large = max_exact + (np.log(nf / max_exact) / math.log(MAX_DISTANCE / max_exact)
                         * (NUM_BUCKETS - max_exact)).astype(np.int32)
    bucket = np.where(n < max_exact, n, np.minimum(large, NUM_BUCKETS - 1))
    return [int(np.argmax(bucket >= b)) for b in range(NUM_BUCKETS)]


T5_THRESHOLDS = _t5_thresholds()
assert T5_THRESHOLDS[-1] <= TK // 2


def _params(*sem):
    return pltpu.CompilerParams(dimension_semantics=sem, vmem_limit_bytes=VMEM_LIMIT)


def _ln_kernel(x_ref, g_ref, b_ref, o_ref, ob_ref):
    z = x_ref[...]
    mu = jnp.mean(z, axis=-1, keepdims=True)
    zc = z - mu
    var = jnp.mean(zc * zc, axis=-1, keepdims=True)
    out = zc * lax.rsqrt(var + LN_EPS) * g_ref[...] + b_ref[...]
    o_ref[...] = out
    ob_ref[...] = out.astype(jnp.bfloat16)


def _layer_norm(x, g, b, rows=256):
    m, d = x.shape
    row_spec = pl.BlockSpec((rows, d), lambda i: (i, 0))
    vec_spec = pl.BlockSpec((1, d), lambda i: (0, 0))
    return pl.pallas_call(
        _ln_kernel,
        grid=(m // rows,),
        in_specs=[row_spec, vec_spec, vec_spec],
        out_specs=[row_spec, row_spec],
        out_shape=[jax.ShapeDtypeStruct((m, d), jnp.float32), jax.ShapeDtypeStruct((m, d), jnp.bfloat16)],
        compiler_params=_params("parallel"),
        name="layer_norm",
    )(x, g.reshape(1, d), b.reshape(1, d))


def _mm_kernel(*refs, relu2, res_scale):
    if res_scale is None:
        *a_refs, w_ref, o_ref, wb_ref = refs
    else:
        *a_refs, w_ref, res_ref, o_ref, wb_ref = refs

    @pl.when(pl.program_id(1) == 0)
    def _():
        wb_ref[...] = w_ref[...].astype(jnp.bfloat16)

    acc, row = None, 0
    for a_ref in a_refs:
        part = jnp.dot(a_ref[...], wb_ref[row:row + a_ref.shape[1], :], preferred_element_type=jnp.float32)
        acc = part if acc is None else acc + part
        row += a_ref.shape[1]
    if relu2:
        acc = jnp.maximum(acc, 0.0)
        acc = acc * acc
    if res_scale is not None:
        acc = res_scale * res_ref[...] + acc
    o_ref[...] = acc.astype(o_ref.dtype)


def _mm_acc_kernel(a_ref, w_ref, res_ref, o_ref, *, res_scale):
    @pl.when(pl.program_id(2) == 0)
    def _():
        o_ref[...] = res_scale * res_ref[...]

    o_ref[...] += jnp.dot(a_ref[...], w_ref[...], preferred_element_type=jnp.float32)


def _matmul_f32w(a_parts, w, layer, n, out_dtype, relu2=False, residual=None, res_scale=None, tm=1024, tn=512):
    m = a_parts[0].shape[0]
    k = sum(a.shape[1] for a in a_parts)
    tn = min(tn, n)
    assert n % tn == 0 and m % tm == 0 and k == w.shape[1]
    res = [] if residual is None else [residual]
    return pl.pallas_call(
        functools.partial(_mm_kernel, relu2=relu2, res_scale=res_scale),
        grid=(n // tn, m // tm),
        in_specs=([pl.BlockSpec((tm, a.shape[1]), lambda j, i: (i, 0)) for a in a_parts]
                  + [pl.BlockSpec((None, k, tn), lambda j, i: (layer, 0, j))]
                  + [pl.BlockSpec((tm, tn), lambda j, i: (i, j)) for _ in res]),
        out_specs=pl.BlockSpec((tm, tn), lambda j, i: (i, j)),
        out_shape=jax.ShapeDtypeStruct((m, n), out_dtype),
        scratch_shapes=[pltpu.VMEM((k, tn), jnp.bfloat16)],
        compiler_params=_params("parallel", "arbitrary"),
        name="matmul",
    )(*a_parts, w, *res)


def _mm_bf16_kernel(a_ref, w_ref, o_ref):
    o_ref[...] = jnp.dot(a_ref[...], w_ref[...], preferred_element_type=jnp.float32).astype(o_ref.dtype)


def _matmul_bf16w(a, w, layer, n, out_dtype, tm=1024, tn=1024):
    m, k = a.shape
    return pl.pallas_call(
        _mm_bf16_kernel,
        grid=(m // tm, n // tn),
        in_specs=[pl.BlockSpec((tm, k), lambda i, j: (i, 0)),
                  pl.BlockSpec((None, k, tn), lambda i, j: (layer, 0, j))],
        out_specs=pl.BlockSpec((tm, tn), lambda i, j: (i, j)),
        out_shape=jax.ShapeDtypeStruct((m, n), out_dtype),
        compiler_params=_params("parallel", "parallel"),
        name="matmul_bf16w",
    )(a, w)


def _matmul_acc(a, w, layer, residual, res_scale, tm=1024, tn=1024, tk=4096):
    m, k = a.shape
    n = w.shape[2]
    return pl.pallas_call(
        functools.partial(_mm_acc_kernel, res_scale=res_scale),
        grid=(m // tm, n // tn, k // tk),
        in_specs=[pl.BlockSpec((tm, tk), lambda i, j, l: (i, l)),
                  pl.BlockSpec((None, tk, tn), lambda i, j, l: (layer, l, j)),
                  pl.BlockSpec((tm, tn), lambda i, j, l: (i, j))],
        out_specs=pl.BlockSpec((tm, tn), lambda i, j, l: (i, j)),
        out_shape=jax.ShapeDtypeStruct((m, n), jnp.float32),
        compiler_params=_params("parallel", "parallel", "arbitrary"),
        name="matmul_acc",
    )(a, w, residual)


def _bias_tiles_kernel(tab_ref, o_ref):
    col = pl.program_id(0)
    row = lax.broadcasted_iota(jnp.int32, (TQ, TK), 0)
    key = lax.broadcasted_iota(jnp.int32, (TQ, TK), 1)
    last = tab_ref[NUM_BUCKETS - 1, col]
    for tile, shift in ((0, 0), (1, TK)):
        dist = row - key + shift
        bias = jnp.full((TQ, TK), tab_ref[0, col], jnp.float32)
        for b in range(1, NUM_BUCKETS):
            bias = jnp.where(dist >= T5_THRESHOLDS[b], tab_ref[b, col], bias)
        o_ref[0, tile] = jnp.where(dist >= 0, (bias - last) * (HEAD_DIM ** 0.5), NEG_INF)


def _bias_tiles(rel_bias):
    ncols = rel_bias.shape[1]
    return pl.pallas_call(
        _bias_tiles_kernel,
        grid=(ncols,),
        in_specs=[pl.BlockSpec(memory_space=pltpu.SMEM)],
        out_specs=pl.BlockSpec((1, 2, TQ, TK), lambda c: (c, 0, 0, 0)),
        out_shape=jax.ShapeDtypeStruct((ncols, 2, TQ, TK), jnp.float32),
        compiler_params=_params("parallel"),
        name="t5_bias_tiles",
    )(rel_bias)


def _nt_dot(a, b):
    return lax.dot_general(a, b, (((1,), (1,)), ((), ())), preferred_element_type=jnp.float32)


def _lanes(x, n):
    return x if n == LANES else jnp.concatenate([x] * (n // LANES), axis=1)


def _with_ones(v):
    return jnp.concatenate([v, jnp.ones((v.shape[0], LANES), v.dtype)], axis=1)


def _softmax_tile(z, v, m_ref, acc_ref, l_ref=None):
    m_old = m_ref[...]
    m_new = jnp.maximum(m_old, jnp.max(z, axis=-1, keepdims=True))
    alpha = jnp.exp2((m_old - m_new) * EXP2_SCALE)
    p = jnp.exp2((z - _lanes(m_new, z.shape[1])) * EXP2_SCALE)
    if l_ref is not None:
        l_ref[...] = alpha * l_ref[...] + jnp.sum(p, axis=-1, keepdims=True)
    acc_ref[...] = _lanes(alpha, acc_ref.shape[1]) * acc_ref[...] + jnp.dot(
        p.astype(jnp.bfloat16), v, preferred_element_type=jnp.float32)
    m_ref[...] = m_new


def _init_state(m_ref, acc_ref):
    m_ref[...] = jnp.full_like(m_ref, NEG_INF)
    acc_ref[...] = jnp.zeros_like(acc_ref)


def _normalized(acc_ref, dv):
    acc = acc_ref[...]
    return acc[:, :dv] / _lanes(acc[:, dv:], dv)


def _key_rows(k0, ntiles):
    return pl.ds(pl.multiple_of(k0 * TK, TK), ntiles * TK)


def _add_near_bias(z, bias_ref, col, near):
    if near == 0:
        return z
    if near == 1:
        return z + bias_ref[col, 0]
    split = z.shape[1] - 2 * TK
    z_near = z[:, split:] + jnp.concatenate([bias_ref[col, 1], bias_ref[col, 0]], axis=1)
    return z_near if split == 0 else jnp.concatenate([z[:, :split], z_near], axis=1)


def _causal_key_loop(qi, logits_fn, consume_fn):
    n_far = jnp.maximum(qi - 1, 0)
    n_big = n_far // FAR_TILES
    rem = n_far - n_big * FAR_TILES

    def tile_fn(k0, ntiles, near):
        consume_fn(k0, ntiles, logits_fn(k0, ntiles, near))

    def big_body(kb, carry):
        tile_fn(kb * FAR_TILES, FAR_TILES, 0)
        return carry

    lax.fori_loop(0, n_big, big_body, 0)

    for left in range(FAR_TILES):
        @pl.when(jnp.logical_and(qi >= 1, rem == left))
        def _(left=left):
            tile_fn(qi - 1 - left, left + 2, 2)

    @pl.when(qi == 0)
    def _():
        tile_fn(qi, 1, 1)


def _diff_kernel(q_ref, k_ref, v_ref, bias_ref, lam_ref, g_ref, o_ref, m_ref, l_ref, acc_ref, *, lambda_init):
    qi = pl.program_id(2)
    nmaps = 2 * DIFF_HEADS_PER_STEP
    for c in range(nmaps):
        _init_state(m_ref.at[c], acc_ref.at[c])
        l_ref[c] = jnp.zeros_like(l_ref[c])

    def logits(k0, ntiles, near):
        rows = _key_rows(k0, ntiles)
        out = []
        for c in range(nmaps):
            cols = slice(c * HEAD_DIM, (c + 1) * HEAD_DIM)
            z = _nt_dot(q_ref[:, cols], k_ref[rows, cols])
            out.append(_add_near_bias(z, bias_ref, c, near))
        return tuple(out)

    def consume(k0, ntiles, zs):
        rows = _key_rows(k0, ntiles)
        for c in range(nmaps):
            hh = c // 2
            _softmax_tile(zs[c], v_ref[rows, hh * DIFF_DV:(hh + 1) * DIFF_DV], m_ref.at[c], acc_ref.at[c],
                          l_ref.at[c])

    _causal_key_loop(qi, logits, consume)

    lv = lam_ref[...]
    lam = (jnp.exp(jnp.sum(lv[0:1] * lv[1:2], axis=-1, keepdims=True))
           - jnp.exp(jnp.sum(lv[2:3] * lv[3:4], axis=-1, keepdims=True)) + lambda_init)
    for hh in range(DIFF_HEADS_PER_STEP):
        a1 = acc_ref[2 * hh] / _lanes(l_ref[2 * hh], DIFF_DV)
        a2 = acc_ref[2 * hh + 1] / _lanes(l_ref[2 * hh + 1], DIFF_DV)
        o = a1 - lam * a2
        o = o * lax.rsqrt(jnp.mean(o * o, axis=-1, keepdims=True) + RMS_EPS) * g_ref[...]
        o_ref[:, hh * DIFF_DV:(hh + 1) * DIFF_DV] = (o * (1.0 - lambda_init)).astype(o_ref.dtype)


def _diff_attention(proj, bias_tiles, lam_vecs, subln_g, lambda_init, batch, seq):
    assert DIFF_HEADS % DIFF_HEADS_PER_STEP == 0
    nq = seq // TQ
    qk_w = 2 * DIFF_HEADS_PER_STEP * HEAD_DIM
    v_w = DIFF_HEADS_PER_STEP * DIFF_DV
    return pl.pallas_call(
        functools.partial(_diff_kernel, lambda_init=lambda_init),
        grid=(batch, DIFF_HEADS // DIFF_HEADS_PER_STEP, nq),
        in_specs=[
            pl.BlockSpec((TQ, qk_w), lambda b, h, i: (b * nq + i, OFF_DQ // qk_w + h)),
            pl.BlockSpec((seq, qk_w), lambda b, h, i: (b, OFF_DK // qk_w + h)),
            pl.BlockSpec((seq, v_w), lambda b, h, i: (b, OFF_DV // v_w + h)),
            pl.BlockSpec((2 * DIFF_HEADS_PER_STEP, 2, TQ, TK), lambda b, h, i: (h, 0, 0, 0)),
            pl.BlockSpec((4, HEAD_DIM), lambda b, h, i: (0, 0)),
            pl.BlockSpec((1, DIFF_DV), lambda b, h, i: (0, 0)),
        ],
        out_specs=pl.BlockSpec((TQ, v_w), lambda b, h, i: (b * nq + i, h)),
        out_shape=jax.ShapeDtypeStruct((batch * seq, DIFF_HEADS * DIFF_DV), jnp.bfloat16),
        scratch_shapes=[pltpu.VMEM((2 * DIFF_HEADS_PER_STEP, TQ, LANES), jnp.float32),
                        pltpu.VMEM((2 * DIFF_HEADS_PER_STEP, TQ, LANES), jnp.float32),
                        pltpu.VMEM((2 * DIFF_HEADS_PER_STEP, TQ, DIFF_DV), jnp.float32)],
        compiler_params=_params("parallel", "parallel", "arbitrary"),
        name="diff_attention",
    )(proj, proj, proj, bias_tiles, lam_vecs, subln_g.reshape(1, DIFF_DV))


def _moba_block_penalty(q, kmean, qi, nblocks):
    nrows = -(-nblocks // 8) * 8
    kmean_hi = kmean.astype(jnp.bfloat16)
    kmean_lo = (kmean - kmean_hi.astype(jnp.float32)).astype(jnp.bfloat16)
    gate = (_nt_dot(kmean_hi, q) + _nt_dot(kmean_lo, q))[:nrows]
    blk = lax.broadcasted_iota(jnp.int32, (nrows, TQ), 0)
    gate = jnp.where(blk < qi, gate, NEG_INF)
    beaten = jnp.zeros((nrows, TQ), jnp.float32)
    for n in range(nblocks):
        other = gate[n:n + 1, :]
        wins = jnp.where(other > gate, 1.0, jnp.where(other == gate, jnp.where(blk > n, 1.0, 0.0), 0.0))
        beaten = beaten + wins
    past_pen = jnp.where(beaten < MOBA_TOPK, 0.0, NEG_INF)
    own_pen = jnp.where(blk == qi, 0.0, NEG_INF)
    pen_t = jnp.where(blk < qi, past_pen, own_pen)
    pen_t = jnp.concatenate([pen_t, jnp.full((LANES - nrows, TQ), NEG_INF, jnp.float32)], axis=0)
    return pen_t.T


def _moba_kernel(*refs, nblocks):
    nh = HEADS_PER_STEP
    q_refs, k_refs, v_refs = refs[:nh], refs[nh:2 * nh], refs[2 * nh:3 * nh]
    bias_ref, o_ref, m_ref, acc_ref, kmean_ref = refs[3 * nh:]
    qi = pl.program_id(2)

    @pl.when(qi == 0)
    def _():
        kmean_ref[...] = jnp.zeros_like(kmean_ref)
        for hh in range(nh):
            for n in range(nblocks):
                blk = k_refs[hh][n * MOBA_BLOCK:(n + 1) * MOBA_BLOCK, :].astype(jnp.float32)
                kmean_ref[hh, n:n + 1, :] = jnp.mean(blk, axis=0, keepdims=True)

    q_aug = []
    for hh in range(nh):
        _init_state(m_ref.at[hh], acc_ref.at[hh])
        q = q_refs[hh][...]
        pen = _moba_block_penalty(q, kmean_ref[hh], qi, nblocks)
        q_aug.append(jnp.concatenate([q, pen.astype(jnp.bfloat16)], axis=1))

    def logits(k0, ntiles, near):
        rows = _key_rows(k0, ntiles)
        n = ntiles * TK
        block_of_key = k0 + lax.broadcasted_iota(jnp.int32, (n, LANES), 0) // MOBA_BLOCK
        one_hot = jnp.where(lax.broadcasted_iota(jnp.int32, (n, LANES), 1) == block_of_key, 1.0, 0.0)
        one_hot = one_hot.astype(jnp.bfloat16)
        out = []
        for hh in range(nh):
            z = _nt_dot(q_aug[hh], jnp.concatenate([k_refs[hh][rows, :], one_hot], axis=1))
            out.append(_add_near_bias(z, bias_ref, hh, near))
        return tuple(out)

    def consume(k0, ntiles, zs):
        rows = _key_rows(k0, ntiles)
        for hh in range(nh):
            _softmax_tile(zs[hh], _with_ones(v_refs[hh][rows, :]), m_ref.at[hh], acc_ref.at[hh])

    _causal_key_loop(qi, logits, consume)
    for hh in range(nh):
        o_ref[:, hh * HEAD_DIM:(hh + 1) * HEAD_DIM] = _normalized(acc_ref.at[hh], HEAD_DIM).astype(o_ref.dtype)


def _head_specs(block_rows, row_index, col_off):
    return [pl.BlockSpec((block_rows, HEAD_DIM),
                         lambda b, h, i, hh=hh: (row_index(b, i), col_off // HEAD_DIM + h * HEADS_PER_STEP + hh))
            for hh in range(HEADS_PER_STEP)]


def _moba_attention(proj, bias_tiles, batch, seq):
    assert TQ == MOBA_BLOCK and TK == MOBA_BLOCK and seq % MOBA_BLOCK == 0
    assert MOBA_HEADS % HEADS_PER_STEP == 0
    nq = seq // TQ
    nblocks = seq // MOBA_BLOCK
    assert MOBA_TOPK <= nblocks <= LANES
    nh = HEADS_PER_STEP
    w = nh * HEAD_DIM
    q_row = lambda b, i: b * nq + i
    seq_row = lambda b, i: b
    return pl.pallas_call(
        functools.partial(_moba_kernel, nblocks=nblocks),
        grid=(batch, MOBA_HEADS // nh, nq),
        in_specs=(_head_specs(TQ, q_row, OFF_MQ) + _head_specs(seq, seq_row, OFF_MK)
                  + _head_specs(seq, seq_row, OFF_MV)
                  + [pl.BlockSpec((nh, 2, TQ, TK), lambda b, h, i: (h, 0, 0, 0))]),
        out_specs=pl.BlockSpec((TQ, w), lambda b, h, i: (b * nq + i, h)),
        out_shape=jax.ShapeDtypeStruct((batch * seq, MOBA_HEADS * HEAD_DIM), jnp.bfloat16),
        scratch_shapes=[pltpu.VMEM((nh, TQ, LANES), jnp.float32),
                        pltpu.VMEM((nh, TQ, HEAD_DIM + LANES), jnp.float32),
                        pltpu.VMEM((nh, LANES, HEAD_DIM), jnp.float32)],
        compiler_params=_params("parallel", "parallel", "arbitrary"),
        name="moba_attention",
    )(*([proj] * (3 * nh)), bias_tiles)


def _dsa_prep_kernel(x_ref, g_ref, wuk_ref, wuv_ref, k_ref, v_ref, iklo_ref, ikhi_ref, iw_ref):
    ckv = x_ref[:, :KV_LATENT]
    ckv = ckv * lax.rsqrt(jnp.mean(ckv * ckv, axis=-1, keepdims=True) + RMS_EPS) * g_ref[...]
    ckv = ckv.astype(jnp.bfloat16)
    k_ref[...] = jnp.dot(ckv, wuk_ref[...], preferred_element_type=jnp.float32).astype(k_ref.dtype)
    v_ref[...] = jnp.dot(ckv, wuv_ref[...], preferred_element_type=jnp.float32).astype(v_ref.dtype)

    grp = x_ref[:, KV_LATENT:]
    lane = lax.broadcasted_iota(jnp.int32, grp.shape, 1)
    is_key = lane < IDX_DIM
    mu = jnp.sum(jnp.where(is_key, grp, 0.0), axis=-1, keepdims=True) / IDX_DIM
    cen = jnp.where(is_key, grp - mu, 0.0)
    var = jnp.sum(cen * cen, axis=-1, keepdims=True) / IDX_DIM
    key_lo = cen * lax.rsqrt(var + LN_EPS)
    iklo_ref[...] = key_lo.astype(iklo_ref.dtype)
    ikhi_ref[...] = pltpu.roll(key_lo, IDX_DIM, axis=1).astype(ikhi_ref.dtype)
    w = pltpu.roll(grp, LANES - IDX_DIM, axis=1)
    iw_ref[...] = jnp.where(lane < IDX_HEADS, w * (IDX_HEADS ** -0.5) * (IDX_DIM ** -0.5), 0.0)


def _dsa_prep(tail, kv_norm_g, wuk_flat, wuv_flat, rows=512):
    m = tail.shape[0]
    n = DSA_HEADS * HEAD_DIM
    row = lambda w: pl.BlockSpec((rows, w), lambda i: (i, 0))
    full = lambda a: pl.BlockSpec(a.shape, lambda i: (0, 0))
    g = kv_norm_g.reshape(1, KV_LATENT)
    return pl.pallas_call(
        _dsa_prep_kernel,
        grid=(m // rows,),
        in_specs=[row(TAIL_W), full(g), full(wuk_flat), full(wuv_flat)],
        out_specs=[row(n), row(n), row(LANES), row(LANES), row(LANES)],
        out_shape=[jax.ShapeDtypeStruct((m, n), jnp.bfloat16), jax.ShapeDtypeStruct((m, n), jnp.bfloat16),
                   jax.ShapeDtypeStruct((m, LANES), jnp.bfloat16), jax.ShapeDtypeStruct((m, LANES), jnp.bfloat16),
                   jax.ShapeDtypeStruct((m, LANES), jnp.float32)],
        compiler_params=_params("parallel"),
        name="dsa_prep",
    )(tail, g, wuk_flat, wuv_flat)


def _sortable(x):
    b = pltpu.bitcast(x, jnp.int32)
    return b ^ ((b >> 31) & jnp.int32(0x7FFFFFFF))


def _dsa_select_kernel(iq_ref, iklo_ref, ikhi_ref, iwt_ref, o_ref, key_ref, eqidx_ref, cut_ref, *, n_top):
    qi = pl.program_id(1)
    n_ch = qi + 1
    shape = (SEL_CH, SEL_TQ)
    t = qi * SEL_TQ + lax.broadcasted_iota(jnp.int32, shape, 1)
    key_in_chunk = lax.broadcasted_iota(jnp.int32, shape, 0)
    chunk = lambda c: pl.ds(pl.multiple_of(c * SEL_CH, SEL_CH), SEL_CH)

    def score_body(c, carry):
        rows = chunk(c)
        k_lo = iklo_ref[rows, :]
        k_hi = ikhi_ref[rows, :]
        acc = jnp.zeros(shape, jnp.float32)
        for pair in range(IDX_HEADS // 2):
            q2 = iq_ref[:, pair * LANES:(pair + 1) * LANES]
            for half, k_half in enumerate((k_lo, k_hi)):
                hd = 2 * pair + half
                acc = acc + jnp.maximum(_nt_dot(k_half, q2), 0.0) * iwt_ref[hd:hd + 1, :]
        score = jnp.where(c * SEL_CH + key_in_chunk <= t, acc, NEG_INF)
        key_ref[c] = _sortable(score)
        return carry

    lax.fori_loop(0, n_ch, score_body, 0)

    def count(ref, pred):
        def body(c, acc):
            hit = jnp.where(pred(ref[c]), 1.0, 0.0)
            return acc + jnp.sum(hit.reshape(SEL_CH // COUNT_ROWS, COUNT_ROWS, SEL_TQ), axis=0)
        acc = lax.fori_loop(0, n_ch, body, jnp.zeros((COUNT_ROWS, SEL_TQ), jnp.float32))
        return jnp.sum(acc, axis=0, keepdims=True)

    v = jnp.where(count(key_ref, lambda x: x >= 0) >= n_top, jnp.int32(0), jnp.int32(INT_MIN))

    def bit_body(i, v):
        trial = v + jnp.left_shift(jnp.int32(1), 30 - i)
        return jnp.where(count(key_ref, lambda x: x >= trial) >= n_top, trial, v)

    v = lax.fori_loop(0, 31, bit_body, v)

    n_above = count(key_ref, lambda x: x > v)
    need = n_top - n_above
    n_tied = count(key_ref, lambda x: x >= v) - n_above
    far = jnp.int32(2 ** 30)

    def eq_body(c, carry):
        eqidx_ref[c] = jnp.where(key_ref[c] == v, c * SEL_CH + key_in_chunk, far)
        return carry

    lax.fori_loop(0, n_ch, eq_body, 0)
    nbits = max(1, (key_ref.shape[0] * SEL_CH - 1).bit_length())
    cut_ref[...] = jnp.full_like(cut_ref, far - 1)

    @pl.when(jnp.max(jnp.where(n_tied > need, 1.0, 0.0)) > 0.0)
    def _():
        def idx_body(i, cut):
            trial = cut + jnp.left_shift(jnp.int32(1), nbits - 1 - i)
            return jnp.where(count(eqidx_ref, lambda x: x < trial) <= need - 1.0, trial, cut)

        cut_ref[...] = lax.fori_loop(0, nbits, idx_body, jnp.zeros((1, SEL_TQ), jnp.int32))

    cut = cut_ref[...]
    o_ref[...] = jnp.full_like(o_ref, NEG_INF)

    def out_body(c, carry):
        picked = jnp.where(key_ref[c] > v, 0.0, jnp.where(eqidx_ref[c] <= cut, 0.0, NEG_INF))
        o_ref[0, c] = jnp.where(c * SEL_CH + key_in_chunk <= t, picked, NEG_INF).T
        return carry

    lax.fori_loop(0, n_ch, out_body, 0)


def _dsa_select(proj, ik_lo, ik_hi, iw, batch, seq):
    assert SEL_TQ == SEL_CH and seq % SEL_CH == 0 and SEL_CH >= DSA_TOPK
    n_top = min(DSA_TOPK, seq // 4)
    nq = seq // SEL_TQ
    n_ch = seq // SEL_CH
    iq_w = IDX_HEADS * IDX_DIM
    key_spec = pl.BlockSpec((seq, LANES), lambda b, i: (b, 0))
    iw_t = iw[:, :IDX_HEADS].T
    return pl.pallas_call(
        functools.partial(_dsa_select_kernel, n_top=n_top),
        grid=(batch, nq),
        in_specs=[pl.BlockSpec((SEL_TQ, iq_w), lambda b, i: (b * nq + i, OFF_IQ // iq_w)),
                  key_spec, key_spec,
                  pl.BlockSpec((IDX_HEADS, SEL_TQ), lambda b, i: (0, b * nq + i))],
        out_specs=pl.BlockSpec((1, n_ch, SEL_TQ, SEL_CH), lambda b, i: (b, 0, i, 0)),
        out_shape=jax.ShapeDtypeStruct((batch, n_ch, seq, SEL_CH), jnp.float32),
        scratch_shapes=[pltpu.VMEM((n_ch, SEL_CH, SEL_TQ), jnp.int32), pltpu.VMEM((n_ch, SEL_CH, SEL_TQ), jnp.int32),
                        pltpu.VMEM((1, SEL_TQ), jnp.int32)],
        compiler_params=_params("parallel", "arbitrary"),
        name="dsa_select",
    )(proj, ik_lo, ik_hi, iw_t)


def _dsa_attn_kernel(*refs):
    nh = HEADS_PER_STEP
    q_refs = refs[:nh]
    k_ref, v_ref, bias_ref, mask_ref, o_ref, m_ref, acc_ref = refs[nh:]
    qi = pl.program_id(2)
    heads = [slice(hh * HEAD_DIM, (hh + 1) * HEAD_DIM) for hh in range(nh)]
    for hh in range(nh):
        _init_state(m_ref.at[hh], acc_ref.at[hh])

    def logits(k0, ntiles, near):
        rows = _key_rows(k0, ntiles)
        mask = mask_ref[0, k0] if ntiles == 1 else jnp.concatenate(
            [mask_ref[0, k0 + j] for j in range(ntiles)], axis=1)
        out = []
        for hh, cols in enumerate(heads):
            z = _nt_dot(q_refs[hh][...], k_ref[rows, cols]) + mask
            out.append(_add_near_bias(z, bias_ref, hh, near))
        return tuple(out)

    def consume(k0, ntiles, zs):
        rows = _key_rows(k0, ntiles)
        for hh, cols in enumerate(heads):
            _softmax_tile(zs[hh], _with_ones(v_ref[rows, cols]), m_ref.at[hh], acc_ref.at[hh])

    _causal_key_loop(qi, logits, consume)
    for hh, cols in enumerate(heads):
        o_ref[:, cols] = _normalized(acc_ref.at[hh], HEAD_DIM).astype(o_ref.dtype)


def _dsa_attention(proj, k_dsa, v_dsa, mask, bias_tiles, batch, seq):
    assert DSA_HEADS % HEADS_PER_STEP == 0
    nq = seq // TQ
    nh = HEADS_PER_STEP
    w = nh * HEAD_DIM
    kv_spec = pl.BlockSpec((seq, w), lambda b, h, i: (b, h))
    return pl.pallas_call(
        _dsa_attn_kernel,
        grid=(batch, DSA_HEADS // nh, nq),
        in_specs=(_head_specs(TQ, lambda b, i: b * nq + i, OFF_CQ) + [
            kv_spec, kv_spec,
            pl.BlockSpec((nh, 2, TQ, TK), lambda b, h, i: (h, 0, 0, 0)),
            pl.BlockSpec((1, seq // TK, TQ, TK), lambda b, h, i: (b, 0, i, 0)),
        ]),
        out_specs=pl.BlockSpec((TQ, w), lambda b, h, i: (b * nq + i, h)),
        out_shape=jax.ShapeDtypeStruct((batch * seq, DSA_HEADS * HEAD_DIM), jnp.bfloat16),
        scratch_shapes=[pltpu.VMEM((nh, TQ, LANES), jnp.float32),
                        pltpu.VMEM((nh, TQ, HEAD_DIM + LANES), jnp.float32)],
        compiler_params=_params("parallel", "parallel", "arbitrary"),
        name="dsa_attention",
    )(*([proj] * nh), k_dsa, v_dsa, bias_tiles, mask)


def _mixers(proj, tail, bias_tiles, diff_lambda, diff_subln_g, kv_norm_g, w_uk, w_uv,
            layer_idx, batch, seq):
    lambda_init = 0.8 - 0.6 * math.exp(-0.3 * layer_idx)
    c0 = 2 * DIFF_HEADS
    c1 = c0 + MOBA_HEADS
    flat = lambda w: w.transpose(1, 0, 2).reshape(KV_LATENT, DSA_HEADS * HEAD_DIM).astype(jnp.bfloat16)
    y_diff = _diff_attention(proj, bias_tiles, diff_lambda, diff_subln_g, lambda_init, batch, seq)
    y_moba = _moba_attention(proj, bias_tiles[c0:c1], batch, seq)
    k_dsa, v_dsa, ik_lo, ik_hi, iw = _dsa_prep(tail, kv_norm_g, flat(w_uk), flat(w_uv))
    mask = _dsa_select(proj, ik_lo, ik_hi, iw, batch, seq)
    y_dsa = _dsa_attention(proj, k_dsa, v_dsa, mask, bias_tiles[c1:], batch, seq)
    return [y_diff, y_moba, y_dsa]


def _tail_weights(w_in_b):
    pad = jnp.zeros(w_in_b.shape[:2] + (TAIL_W - (D_IN - OFF_IK) - KV_LATENT,), w_in_b.dtype)
    return jnp.concatenate([w_in_b[:, :, OFF_CKV:OFF_IQ], w_in_b[:, :, OFF_IK:], pad], axis=2)


def kernel(x, ln_emb_g, ln_emb_b, rel_bias, w_in, diff_lambda, diff_subln_g, kv_norm_g, w_uk, w_uv, w_o,
           ln1_g, ln1_b, w_up, w_down, ln2_g, ln2_b):
    batch, seq, d_model = x.shape
    depth = w_in.shape[0]
    alpha = (2.0 * depth) ** 0.25
    bf16 = jnp.bfloat16
    bias_tiles = _bias_tiles(rel_bias)
    w_down_b = w_down.astype(bf16)
    w_in_b = w_in.astype(bf16)
    w_tail_b = _tail_weights(w_in_b)
    h, hb = _layer_norm(x.reshape(batch * seq, d_model), ln_emb_g, ln_emb_b)
    for l in range(depth):
        proj = _matmul_bf16w(hb, w_in_b, l, N_MAIN, bf16)
        tail = _matmul_bf16w(hb, w_tail_b, l, TAIL_W, jnp.float32, tn=TAIL_W)
        mix = _mixers(proj, tail, bias_tiles, diff_lambda[l], diff_subln_g[l], kv_norm_g[l],
                      w_uk[l], w_uv[l], l, batch, seq)
        z = _matmul_f32w(mix, w_o, l, d_model, jnp.float32, residual=h, res_scale=alpha)
        h, hb = _layer_norm(z, ln1_g[l], ln1_b[l])
        up = _matmul_f32w([hb], w_up, l, w_up.shape[2], bf16, relu2=True)
        z = _matmul_acc(up, w_down_b, l, h, alpha)
        h, hb = _layer_norm(z, ln2_g[l], ln2_b[l])
    return h.reshape(batch, seq, d_model)
```

```python
import functools
import math

import jax
import jax.numpy as jnp
import numpy as np
from jax import lax
from jax.experimental import pallas as pl
from jax.experimental.pallas import tpu as pltpu

HEAD_DIM = 128
DIFF_HEADS = 6
DIFF_DV = 2 * HEAD_DIM
MOBA_HEADS = 10
DSA_HEADS = 10
MOBA_BLOCK = 256
MOBA_TOPK = 3
DSA_TOPK = 256
KV_LATENT = 512
IDX_HEADS = 32
IDX_DIM = 64
NUM_BUCKETS = 32
MAX_DISTANCE = 128
LN_EPS = 1e-5
RMS_EPS = 1e-5
NEG_INF = -1e30

OFF_DQ = 0
OFF_DK = OFF_DQ + 2 * DIFF_HEADS * HEAD_DIM
OFF_DV = OFF_DK + 2 * DIFF_HEADS * HEAD_DIM
OFF_MQ = OFF_DV + DIFF_HEADS * DIFF_DV
OFF_MK = OFF_MQ + MOBA_HEADS * HEAD_DIM
OFF_MV = OFF_MK + MOBA_HEADS * HEAD_DIM
OFF_CQ = OFF_MV + MOBA_HEADS * HEAD_DIM
OFF_CKV = OFF_CQ + DSA_HEADS * HEAD_DIM
OFF_IQ = OFF_CKV + KV_LATENT
OFF_IK = OFF_IQ + IDX_HEADS * IDX_DIM
OFF_IW = OFF_IK + IDX_DIM
D_IN = OFF_IW + IDX_HEADS
N_MAIN = OFF_IK
TAIL_W = KV_LATENT + 128

LANES = 128
VMEM_LIMIT = 56 * 1024 * 1024
TQ = 256
TK = 256
SEL_TQ = TQ
SEL_CH = TK
COUNT_ROWS = 32
INT_MIN = -2 ** 31
FAR_TILES = 6
HEADS_PER_STEP = 5
DIFF_HEADS_PER_STEP = 3
EXP2_SCALE = HEAD_DIM ** -0.5 * math.log2(math.e)


def _t5_thresholds():
    n = np.arange(0, 4 * MAX_DISTANCE)
    max_exact = NUM_BUCKETS // 2
    nf = np.maximum(n, 1).astype(np.float64)
    large = max_exact + (np.log(nf / max_exact) / math.log(MAX_DISTANCE / max_exact)
                         * (NUM_BUCKETS - max_exact)).astype(np.int32)
    bucket = np.where(n < max_exact, n, np.minimum(large, NUM_BUCKETS - 1))
    return [int(np.argmax(bucket >= b)) for b in range(NUM_BUCKETS)]


T5_THRESHOLDS = _t5_thresholds()
assert T5_THRESHOLDS[-1] <= TK // 2


def _params(*sem):
    return pltpu.CompilerParams(dimension_semantics=sem, vmem_limit_bytes=VMEM_LIMIT)


def _ln_kernel(x_ref, g_ref, b_ref, o_ref, ob_ref):
    z = x_ref[...]
    mu = jnp.mean(z, axis=-1, keepdims=True)
    zc = z - mu
    var = jnp.mean(zc * zc, axis=-1, keepdims=True)
    out = zc * lax.rsqrt(var + LN_EPS) * g_ref[...] + b_ref[...]
    o_ref[...] = out
    ob_ref[...] = out.astype(jnp.bfloat16)


def _layer_norm(x, g, b, rows=256):
    m, d = x.shape
    row_spec = pl.BlockSpec((rows, d), lambda i: (i, 0))
    vec_spec = pl.BlockSpec((1, d), lambda i: (0, 0))
    return pl.pallas_call(
        _ln_kernel,
        grid=(m // rows,),
        in_specs=[row_spec, vec_spec, vec_spec],
        out_specs=[row_spec, row_spec],
        out_shape=[jax.ShapeDtypeStruct((m, d), jnp.float32), jax.ShapeDtypeStruct((m, d), jnp.bfloat16)],
        compiler_params=_params("parallel"),
        name="layer_norm",
    )(x, g.reshape(1, d), b.reshape(1, d))


def _mm_kernel(*refs, relu2, res_scale):
    if res_scale is None:
        *a_refs, w_ref, o_ref, wb_ref = refs
    else:
        *a_refs, w_ref, res_ref, o_ref, wb_ref = refs

    @pl.when(pl.program_id(1) == 0)
    def _():
        wb_ref[...] = w_ref[...].astype(jnp.bfloat16)

    acc, row = None, 0
    for a_ref in a_refs:
        part = jnp.dot(a_ref[...], wb_ref[row:row + a_ref.shape[1], :], preferred_element_type=jnp.float32)
        acc = part if acc is None else acc + part
        row += a_ref.shape[1]
    if relu2:
        acc = jnp.maximum(acc, 0.0)
        acc = acc * acc
    if res_scale is not None:
        acc = res_scale * res_ref[...] + acc
    o_ref[...] = acc.astype(o_ref.dtype)


def _mm_acc_kernel(a_ref, w_ref, res_ref, o_ref, *, res_scale):
    @pl.when(pl.program_id(2) == 0)
    def _():
        o_ref[...] = res_scale * res_ref[...]

    o_ref[...] += jnp.dot(a_ref[...], w_ref[...], preferred_element_type=jnp.float32)


def _matmul_f32w(a_parts, w, layer, n, out_dtype, relu2=False, residual=None, res_scale=None, tm=1024, tn=512):
    m = a_parts[0].shape[0]
    k = sum(a.shape[1] for a in a_parts)
    tn = min(tn, n)
    assert n % tn == 0 and m % tm == 0 and k == w.shape[1]
    res = [] if residual is None else [residual]
    return pl.pallas_call(
        functools.partial(_mm_kernel, relu2=relu2, res_scale=res_scale),
        grid=(n // tn, m // tm),
        in_specs=([pl.BlockSpec((tm, a.shape[1]), lambda j, i: (i, 0)) for a in a_parts]
                  + [pl.BlockSpec((None, k, tn), lambda j, i: (layer, 0, j))]
                  + [pl.BlockSpec((tm, tn), lambda j, i: (i, j)) for _ in res]),
        out_specs=pl.BlockSpec((tm, tn), lambda j, i: (i, j)),
        out_shape=jax.ShapeDtypeStruct((m, n), out_dtype),
        scratch_shapes=[pltpu.VMEM((k, tn), jnp.bfloat16)],
        compiler_params=_params("parallel", "arbitrary"),
        name="matmul",
    )(*a_parts, w, *res)


def _mm_bf16_kernel(a_ref, w_ref, o_ref):
    o_ref[...] = jnp.dot(a_ref[...], w_ref[...], preferred_element_type=jnp.float32).astype(o_ref.dtype)


def _matmul_bf16w(a, w, layer, n, out_dtype, tm=1024, tn=1024):
    m, k = a.shape
    return pl.pallas_call(
        _mm_bf16_kernel,
        grid=(m // tm, n // tn),
        in_specs=[pl.BlockSpec((tm, k), lambda i, j: (i, 0)),
                  pl.BlockSpec((None, k, tn), lambda i, j: (layer, 0, j))],
        out_specs=pl.BlockSpec((tm, tn), lambda i, j: (i, j)),
        out_shape=jax.ShapeDtypeStruct((m, n), out_dtype),
        compiler_params=_params("parallel", "parallel"),
        name="matmul_bf16w",
    )(a, w)


def _matmul_acc(a, w, layer, residual, res_scale, tm=1024, tn=1024, tk=4096):
    m, k = a.shape
    n = w.shape[2]
    return pl.pallas_call(
        functools.partial(_mm_acc_kernel, res_scale=res_scale),
        grid=(m // tm, n // tn, k // tk),
        in_specs=[pl.BlockSpec((tm, tk), lambda i, j, l: (i, l)),
                  pl.BlockSpec((None, tk, tn), lambda i, j, l: (layer, l, j)),
                  pl.BlockSpec((tm, tn), lambda i, j, l: (i, j))],
        out_specs=pl.BlockSpec((tm, tn), lambda i, j, l: (i, j)),
        out_shape=jax.ShapeDtypeStruct((m, n), jnp.float32),
        compiler_params=_params("parallel", "parallel", "arbitrary"),
        name="matmul_acc",
    )(a, w, residual)


def _bias_tiles_kernel(tab_ref, o_ref):
    col = pl.program_id(0)
    row = lax.broadcasted_iota(jnp.int32, (TQ, TK), 0)
    key = lax.broadcasted_iota(jnp.int32, (TQ, TK), 1)
    last = tab_ref[NUM_BUCKETS - 1, col]
    for tile, shift in ((0, 0), (1, TK)):
        dist = row - key + shift
        bias = jnp.full((TQ, TK), tab_ref[0, col], jnp.float32)
        for b in range(1, NUM_BUCKETS):
            bias = jnp.where(dist >= T5_THRESHOLDS[b], tab_ref[b, col], bias)
        o_ref[0, tile] = jnp.where(dist >= 0, (bias - last) * (HEAD_DIM ** 0.5), NEG_INF)


def _bias_tiles(rel_bias):
    ncols = rel_bias.shape[1]
    return pl.pallas_call(
        _bias_tiles_kernel,
        grid=(ncols,),
        in_specs=[pl.BlockSpec(memory_space=pltpu.SMEM)],
        out_specs=pl.BlockSpec((1, 2, TQ, TK), lambda c: (c, 0, 0, 0)),
        out_shape=jax.ShapeDtypeStruct((ncols, 2, TQ, TK), jnp.float32),
        compiler_params=_params("parallel"),
        name="t5_bias_tiles",
    )(rel_bias)


def _nt_dot(a, b):
    return lax.dot_general(a, b, (((1,), (1,)), ((), ())), preferred_element_type=jnp.float32)


def _lanes(x, n):
    return x if n == LANES else jnp.concatenate([x] * (n // LANES), axis=1)


def _with_ones(v):
    return jnp.concatenate([v, jnp.ones((v.shape[0], LANES), v.dtype)], axis=1)


def _softmax_tile(z, v, m_ref, acc_ref, l_ref=None):
    m_old = m_ref[...]
    m_new = jnp.maximum(m_old, jnp.max(z, axis=-1, keepdims=True))
    alpha = jnp.exp2((m_old - m_new) * EXP2_SCALE)
    p = jnp.exp2((z - _lanes(m_new, z.shape[1])) * EXP2_SCALE)
    if l_ref is not None:
        l_ref[...] = alpha * l_ref[...] + jnp.sum(p, axis=-1, keepdims=True)
    acc_ref[...] = _lanes(alpha, acc_ref.shape[1]) * acc_ref[...] + jnp.dot(
        p.astype(jnp.bfloat16), v, preferred_element_type=jnp.float32)
    m_ref[...] = m_new


def _init_state(m_ref, acc_ref):
    m_ref[...] = jnp.full_like(m_ref, NEG_INF)
    acc_ref[...] = jnp.zeros_like(acc_ref)


def _normalized(acc_ref, dv):
    acc = acc_ref[...]
    return acc[:, :dv] / _lanes(acc[:, dv:], dv)


def _key_rows(k0, ntiles):
    return pl.ds(pl.multiple_of(k0 * TK, TK), ntiles * TK)


def _add_near_bias(z, bias_ref, col, near):
    if near == 0:
        return z
    if near == 1:
        return z + bias_ref[col, 0]
    split = z.shape[1] - 2 * TK
    z_near = z[:, split:] + jnp.concatenate([bias_ref[col, 1], bias_ref[col, 0]], axis=1)
    return z_near if split == 0 else jnp.concatenate([z[:, :split], z_near], axis=1)


def _causal_key_loop(qi, logits_fn, consume_fn):
    n_far = jnp.maximum(qi - 1, 0)
    n_big = n_far // FAR_TILES
    rem = n_far - n_big * FAR_TILES

    def tile_fn(k0, ntiles, near):
        consume_fn(k0, ntiles, logits_fn(k0, ntiles, near))

    def big_body(kb, carry):
        tile_fn(kb * FAR_TILES, FAR_TILES, 0)
        return carry

    lax.fori_loop(0, n_big, big_body, 0)

    for left in range(FAR_TILES):
        @pl.when(jnp.logical_and(qi >= 1, rem == left))
        def _(left=left):
            tile_fn(qi - 1 - left, left + 2, 2)

    @pl.when(qi == 0)
    def _():
        tile_fn(qi, 1, 1)


def _diff_kernel(q_ref, k_ref, v_ref, bias_ref, lam_ref, g_ref, o_ref, m_ref, l_ref, acc_ref, *, lambda_init):
    qi = pl.program_id(2)
    nmaps = 2 * DIFF_HEADS_PER_STEP
    for c in range(nmaps):
        _init_state(m_ref.at[c], acc_ref.at[c])
        l_ref[c] = jnp.zeros_like(l_ref[c])

    def logits(k0, ntiles, near):
        rows = _key_rows(k0, ntiles)
        out = []
        for c in range(nmaps):
            cols = slice(c * HEAD_DIM, (c + 1) * HEAD_DIM)
            z = _nt_dot(q_ref[:, cols], k_ref[rows, cols])
            out.append(_add_near_bias(z, bias_ref, c, near))
        return tuple(out)

    def consume(k0, ntiles, zs):
        rows = _key_rows(k0, ntiles)
        for c in range(nmaps):
            hh = c // 2
            _softmax_tile(zs[c], v_ref[rows, hh * DIFF_DV:(hh + 1) * DIFF_DV], m_ref.at[c], acc_ref.at[c],
                          l_ref.at[c])

    _causal_key_loop(qi, logits, consume)

    lv = lam_ref[...]
    lam = (jnp.exp(jnp.sum(lv[0:1] * lv[1:2], axis=-1, keepdims=True))
           - jnp.exp(jnp.sum(lv[2:3] * lv[3:4], axis=-1, keepdims=True)) + lambda_init)
    for hh in range(DIFF_HEADS_PER_STEP):
        a1 = acc_ref[2 * hh] / _lanes(l_ref[2 * hh], DIFF_DV)
        a2 = acc_ref[2 * hh + 1] / _lanes(l_ref[2 * hh + 1], DIFF_DV)
        o = a1 - lam * a2
        o = o * lax.rsqrt(jnp.mean(o * o, axis=-1, keepdims=True) + RMS_EPS) * g_ref[...]
        o_ref[:, hh * DIFF_DV:(hh + 1) * DIFF_DV] = (o * (1.0 - lambda_init)).astype(o_ref.dtype)


def _diff_attention(proj, bias_tiles, lam_vecs, subln_g, lambda_init, batch, seq):
    assert DIFF_HEADS % DIFF_HEADS_PER_STEP == 0
    nq = seq // TQ
    qk_w = 2 * DIFF_HEADS_PER_STEP * HEAD_DIM
    v_w = DIFF_HEADS_PER_STEP * DIFF_DV
    return pl.pallas_call(
        functools.partial(_diff_kernel, lambda_init=lambda_init),
        grid=(batch, DIFF_HEADS // DIFF_HEADS_PER_STEP, nq),
        in_specs=[
            pl.BlockSpec((TQ, qk_w), lambda b, h, i: (b * nq + i, OFF_DQ // qk_w + h)),
            pl.BlockSpec((seq, qk_w), lambda b, h, i: (b, OFF_DK // qk_w + h)),
            pl.BlockSpec((seq, v_w), lambda b, h, i: (b, OFF_DV // v_w + h)),
            pl.BlockSpec((2 * DIFF_HEADS_PER_STEP, 2, TQ, TK), lambda b, h, i: (h, 0, 0, 0)),
            pl.BlockSpec((4, HEAD_DIM), lambda b, h, i: (0, 0)),
            pl.BlockSpec((1, DIFF_DV), lambda b, h, i: (0, 0)),
        ],
        out_specs=pl.BlockSpec((TQ, v_w), lambda b, h, i: (b * nq + i, h)),
        out_shape=jax.ShapeDtypeStruct((batch * seq, DIFF_HEADS * DIFF_DV), jnp.bfloat16),
        scratch_shapes=[pltpu.VMEM((2 * DIFF_HEADS_PER_STEP, TQ, LANES), jnp.float32),
                        pltpu.VMEM((2 * DIFF_HEADS_PER_STEP, TQ, LANES), jnp.float32),
                        pltpu.VMEM((2 * DIFF_HEADS_PER_STEP, TQ, DIFF_DV), jnp.float32)],
        compiler_params=_params("parallel", "parallel", "arbitrary"),
        name="diff_attention",
    )(proj, proj, proj, bias_tiles, lam_vecs, subln_g.reshape(1, DIFF_DV))


def _moba_block_penalty(q, kmean, qi, nblocks):
    nrows = -(-nblocks // 8) * 8
    kmean_hi = kmean.astype(jnp.bfloat16)
    kmean_lo = (kmean - kmean_hi.astype(jnp.float32)).astype(jnp.bfloat16)
    gate = (_nt_dot(kmean_hi, q) + _nt_dot(kmean_lo, q))[:nrows]
    blk = lax.broadcasted_iota(jnp.int32, (nrows, TQ), 0)
    gate = jnp.where(blk < qi, gate, NEG_INF)
    beaten = jnp.zeros((nrows, TQ), jnp.float32)
    for n in range(nblocks):
        other = gate[n:n + 1, :]
        wins = jnp.where(other > gate, 1.0, jnp.where(other == gate, jnp.where(blk > n, 1.0, 0.0), 0.0))
        beaten = beaten + wins
    past_pen = jnp.where(beaten < MOBA_TOPK, 0.0, NEG_INF)
    own_pen = jnp.where(blk == qi, 0.0, NEG_INF)
    pen_t = jnp.where(blk < qi, past_pen, own_pen)
    pen_t = jnp.concatenate([pen_t, jnp.full((LANES - nrows, TQ), NEG_INF, jnp.float32)], axis=0)
    return pen_t.T


def _moba_kernel(*refs, nblocks):
    nh = HEADS_PER_STEP
    q_refs, k_refs, v_refs = refs[:nh], refs[nh:2 * nh], refs[2 * nh:3 * nh]
    bias_ref, o_ref, m_ref, acc_ref, kmean_ref = refs[3 * nh:]
    qi = pl.program_id(2)

    @pl.when(qi == 0)
    def _():
        kmean_ref[...] = jnp.zeros_like(kmean_ref)
        for hh in range(nh):
            for n in range(nblocks):
                blk = k_refs[hh][n * MOBA_BLOCK:(n + 1) * MOBA_BLOCK, :].astype(jnp.float32)
                kmean_ref[hh, n:n + 1, :] = jnp.mean(blk, axis=0, keepdims=True)

    q_aug = []
    for hh in range(nh):
        _init_state(m_ref.at[hh], acc_ref.at[hh])
        q = q_refs[hh][...]
        pen = _moba_block_penalty(q, kmean_ref[hh], qi, nblocks)
        q_aug.append(jnp.concatenate([q, pen.astype(jnp.bfloat16)], axis=1))

    def logits(k0, ntiles, near):
        rows = _key_rows(k0, ntiles)
        n = ntiles * TK
        block_of_key = k0 + lax.broadcasted_iota(jnp.int32, (n, LANES), 0) // MOBA_BLOCK
        one_hot = jnp.where(lax.broadcasted_iota(jnp.int32, (n, LANES), 1) == block_of_key, 1.0, 0.0)
        one_hot = one_hot.astype(jnp.bfloat16)
        out = []
        for hh in range(nh):
            z = _nt_dot(q_aug[hh], jnp.concatenate([k_refs[hh][rows, :], one_hot], axis=1))
            out.append(_add_near_bias(z, bias_ref, hh, near))
        return tuple(out)

    def consume(k0, ntiles, zs):
        rows = _key_rows(k0, ntiles)
        for hh in range(nh):
            _softmax_tile(zs[hh], _with_ones(v_refs[hh][rows, :]), m_ref.at[hh], acc_ref.at[hh])

    _causal_key_loop(qi, logits, consume)
    for hh in range(nh):
        o_ref[:, hh * HEAD_DIM:(hh + 1) * HEAD_DIM] = _normalized(acc_ref.at[hh], HEAD_DIM).astype(o_ref.dtype)


def _head_specs(block_rows, row_index, col_off):
    return [pl.BlockSpec((block_rows, HEAD_DIM),
                         lambda b, h, i, hh=hh: (row_index(b, i), col_off // HEAD_DIM + h * HEADS_PER_STEP + hh))
            for hh in range(HEADS_PER_STEP)]


def _moba_attention(proj, bias_tiles, batch, seq):
    assert TQ == MOBA_BLOCK and TK == MOBA_BLOCK and seq % MOBA_BLOCK == 0
    assert MOBA_HEADS % HEADS_PER_STEP == 0
    nq = seq // TQ
    nblocks = seq // MOBA_BLOCK
    assert MOBA_TOPK <= nblocks <= LANES
    nh = HEADS_PER_STEP
    w = nh * HEAD_DIM
    q_row = lambda b, i: b * nq + i
    seq_row = lambda b, i: b
    return pl.pallas_call(
        functools.partial(_moba_kernel, nblocks=nblocks),
        grid=(batch, MOBA_HEADS // nh, nq),
        in_specs=(_head_specs(TQ, q_row, OFF_MQ) + _head_specs(seq, seq_row, OFF_MK)
                  + _head_specs(seq, seq_row, OFF_MV)
                  + [pl.BlockSpec((nh, 2, TQ, TK), lambda b, h, i: (h, 0, 0, 0))]),
        out_specs=pl.BlockSpec((TQ, w), lambda b, h, i: (b * nq + i, h)),
        out_shape=jax.ShapeDtypeStruct((batch * seq, MOBA_HEADS * HEAD_DIM), jnp.bfloat16),
        scratch_shapes=[pltpu.VMEM((nh, TQ, LANES), jnp.float32),
                        pltpu.VMEM((nh, TQ, HEAD_DIM + LANES), jnp.float32),
                        pltpu.VMEM((nh, LANES, HEAD_DIM), jnp.float32)],
        compiler_params=_params("parallel", "parallel", "arbitrary"),
        name="moba_attention",
    )(*([proj] * (3 * nh)), bias_tiles)


def _dsa_prep_kernel(x_ref, g_ref, wuk_ref, wuv_ref, k_ref, v_ref, iklo_ref, ikhi_ref, iw_ref):
    ckv = x_ref[:, :KV_LATENT]
    ckv = ckv * lax.rsqrt(jnp.mean(ckv * ckv, axis=-1, keepdims=True) + RMS_EPS) * g_ref[...]
    ckv = ckv.astype(jnp.bfloat16)
    k_ref[...] = jnp.dot(ckv, wuk_ref[...], preferred_element_type=jnp.float32).astype(k_ref.dtype)
    v_ref[...] = jnp.dot(ckv, wuv_ref[...], preferred_element_type=jnp.float32).astype(v_ref.dtype)

    grp = x_ref[:, KV_LATENT:]
    lane = lax.broadcasted_iota(jnp.int32, grp.shape, 1)
    is_key = lane < IDX_DIM
    mu = jnp.sum(jnp.where(is_key, grp, 0.0), axis=-1, keepdims=True) / IDX_DIM
    cen = jnp.where(is_key, grp - mu, 0.0)
    var = jnp.sum(cen * cen, axis=-1, keepdims=True) / IDX_DIM
    key_lo = cen * lax.rsqrt(var + LN_EPS)
    iklo_ref[...] = key_lo.astype(iklo_ref.dtype)
    ikhi_ref[...] = pltpu.roll(key_lo, IDX_DIM, axis=1).astype(ikhi_ref.dtype)
    w = pltpu.roll(grp, LANES - IDX_DIM, axis=1)
    iw_ref[...] = jnp.where(lane < IDX_HEADS, w * (IDX_HEADS ** -0.5) * (IDX_DIM ** -0.5), 0.0)


def _dsa_prep(tail, kv_norm_g, wuk_flat, wuv_flat, rows=512):
    m = tail.shape[0]
    n = DSA_HEADS * HEAD_DIM
    row = lambda w: pl.BlockSpec((rows, w), lambda i: (i, 0))
    full = lambda a: pl.BlockSpec(a.shape, lambda i: (0, 0))
    g = kv_norm_g.reshape(1, KV_LATENT)
    return pl.pallas_call(
        _dsa_prep_kernel,
        grid=(m // rows,),
        in_specs=[row(TAIL_W), full(g), full(wuk_flat), full(wuv_flat)],
        out_specs=[row(n), row(n), row(LANES), row(LANES), row(LANES)],
        out_shape=[jax.ShapeDtypeStruct((m, n), jnp.bfloat16), jax.ShapeDtypeStruct((m, n), jnp.bfloat16),
                   jax.ShapeDtypeStruct((m, LANES), jnp.bfloat16), jax.ShapeDtypeStruct((m, LANES), jnp.bfloat16),
                   jax.ShapeDtypeStruct((m, LANES), jnp.float32)],
        compiler_params=_params("parallel"),
        name="dsa_prep",
    )(tail, g, wuk_flat, wuv_flat)


def _sortable(x):
    b = pltpu.bitcast(x, jnp.int32)
    return b ^ ((b >> 31) & jnp.int32(0x7FFFFFFF))


def _dsa_select_kernel(iq_ref, iklo_ref, ikhi_ref, iwt_ref, o_ref, key_ref, eqidx_ref, cut_ref, *, n_top):
    qi = pl.program_id(1)
    n_ch = qi + 1
    shape = (SEL_CH, SEL_TQ)
    t = qi * SEL_TQ + lax.broadcasted_iota(jnp.int32, shape, 1)
    key_in_chunk = lax.broadcasted_iota(jnp.int32, shape, 0)
    chunk = lambda c: pl.ds(pl.multiple_of(c * SEL_CH, SEL_CH), SEL_CH)

    def score_body(c, carry):
        rows = chunk(c)
        k_lo = iklo_ref[rows, :]
        k_hi = ikhi_ref[rows, :]
        acc = jnp.zeros(shape, jnp.float32)
        for pair in range(IDX_HEADS // 2):
            q2 = iq_ref[:, pair * LANES:(pair + 1) * LANES]
            for half, k_half in enumerate((k_lo, k_hi)):
                hd = 2 * pair + half
                acc = acc + jnp.maximum(_nt_dot(k_half, q2), 0.0) * iwt_ref[hd:hd + 1, :]
        score = jnp.where(c * SEL_CH + key_in_chunk <= t, acc, NEG_INF)
        key_ref[c] = _sortable(score)
        return carry

    lax.fori_loop(0, n_ch, score_body, 0)

    def count(ref, pred):
        def body(c, acc):
            hit = jnp.where(pred(ref[c]), 1.0, 0.0)
            return acc + jnp.sum(hit.reshape(SEL_CH // COUNT_ROWS, COUNT_ROWS, SEL_TQ), axis=0)
        acc = lax.fori_loop(0, n_ch, body, jnp.zeros((COUNT_ROWS, SEL_TQ), jnp.float32))
        return jnp.sum(acc, axis=0, keepdims=True)

    n_nonneg = count(key_ref, lambda x: x >= 0)
    v = jnp.where(n_nonneg >= n_top, jnp.int32(0), jnp.int32(INT_MIN))
    n_at_least = jnp.where(n_nonneg >= n_top, n_nonneg, (n_ch * SEL_CH).astype(jnp.float32))

    def bit_body(i, state):
        v, n_at_least = state
        trial = v + jnp.left_shift(jnp.int32(1), 30 - i)
        n_trial = count(key_ref, lambda x: x >= trial)
        keep = n_trial >= n_top
        return jnp.where(keep, trial, v), jnp.where(keep, n_trial, n_at_least)

    v, n_at_least = lax.fori_loop(0, 31, bit_body, (v, n_at_least))

    n_above = count(key_ref, lambda x: x > v)
    need = n_top - n_above
    n_tied = n_at_least - n_above
    far = jnp.int32(2 ** 30)

    def eq_body(c, carry):
        eqidx_ref[c] = jnp.where(key_ref[c] == v, c * SEL_CH + key_in_chunk, far)
        return carry

    lax.fori_loop(0, n_ch, eq_body, 0)
    nbits = max(1, (key_ref.shape[0] * SEL_CH - 1).bit_length())
    cut_ref[...] = jnp.full_like(cut_ref, far - 1)

    @pl.when(jnp.max(jnp.where(n_tied > need, 1.0, 0.0)) > 0.0)
    def _():
        def idx_body(i, cut):
            trial = cut + jnp.left_shift(jnp.int32(1), nbits - 1 - i)
            return jnp.where(count(eqidx_ref, lambda x: x < trial) <= need - 1.0, trial, cut)

        cut_ref[...] = lax.fori_loop(0, nbits, idx_body, jnp.zeros((1, SEL_TQ), jnp.int32))

    cut = cut_ref[...]
    o_ref[...] = jnp.full_like(o_ref, NEG_INF)

    def out_body(c, carry):
        picked = jnp.where(key_ref[c] > v, 0.0, jnp.where(eqidx_ref[c] <= cut, 0.0, NEG_INF))
        o_ref[0, c] = jnp.where(c * SEL_CH + key_in_chunk <= t, picked, NEG_INF).T
        return carry

    lax.fori_loop(0, n_ch, out_body, 0)


def _dsa_select(proj, ik_lo, ik_hi, iw, batch, seq):
    assert SEL_TQ == SEL_CH and seq % SEL_CH == 0 and SEL_CH >= DSA_TOPK
    n_top = min(DSA_TOPK, seq // 4)
    nq = seq // SEL_TQ
    n_ch = seq // SEL_CH
    iq_w = IDX_HEADS * IDX_DIM
    key_spec = pl.BlockSpec((seq, LANES), lambda b, i: (b, 0))
    iw_t = iw[:, :IDX_HEADS].T
    return pl.pallas_call(
        functools.partial(_dsa_select_kernel, n_top=n_top),
        grid=(batch, nq),
        in_specs=[pl.BlockSpec((SEL_TQ, iq_w), lambda b, i: (b * nq + i, OFF_IQ // iq_w)),
                  key_spec, key_spec,
                  pl.BlockSpec((IDX_HEADS, SEL_TQ), lambda b, i: (0, b * nq + i))],
        out_specs=pl.BlockSpec((1, n_ch, SEL_TQ, SEL_CH), lambda b, i: (b, 0, i, 0)),
        out_shape=jax.ShapeDtypeStruct((batch, n_ch, seq, SEL_CH), jnp.float32),
        scratch_shapes=[pltpu.VMEM((n_ch, SEL_CH, SEL_TQ), jnp.int32), pltpu.VMEM((n_ch, SEL_CH, SEL_TQ), jnp.int32),
                        pltpu.VMEM((1, SEL_TQ), jnp.int32)],
        compiler_params=_params("parallel", "arbitrary"),
        name="dsa_select",
    )(proj, ik_lo, ik_hi, iw_t)


def _dsa_attn_kernel(*refs):
    nh = HEADS_PER_STEP
    q_refs = refs[:nh]
    k_ref, v_ref, bias_ref, mask_ref, o_ref, m_ref, acc_ref = refs[nh:]
    qi = pl.program_id(2)
    heads = [slice(hh * HEAD_DIM, (hh + 1) * HEAD_DIM) for hh in range(nh)]
    for hh in range(nh):
        _init_state(m_ref.at[hh], acc_ref.at[hh])

    def logits(k0, ntiles, near):
        rows = _key_rows(k0, ntiles)
        mask = mask_ref[0, k0] if ntiles == 1 else jnp.concatenate(
            [mask_ref[0, k0 + j] for j in range(ntiles)], axis=1)
        out = []
        for hh, cols in enumerate(heads):
            z = _nt_dot(q_refs[hh][...], k_ref[rows, cols]) + mask
            out.append(_add_near_bias(z, bias_ref, hh, near))
        return tuple(out)

    def consume(k0, ntiles, zs):
        rows = _key_rows(k0, ntiles)
        for hh, cols in enumerate(heads):
            _softmax_tile(zs[hh], _with_ones(v_ref[rows, cols]), m_ref.at[hh], acc_ref.at[hh])

    _causal_key_loop(qi, logits, consume)
    for hh, cols in enumerate(heads):
        o_ref[:, cols] = _normalized(acc_ref.at[hh], HEAD_DIM).astype(o_ref.dtype)


def _dsa_attention(proj, k_dsa, v_dsa, mask, bias_tiles, batch, seq):
    assert DSA_HEADS % HEADS_PER_STEP == 0
    nq = seq // TQ
    nh = HEADS_PER_STEP
    w = nh * HEAD_DIM
    kv_spec = pl.BlockSpec((seq, w), lambda b, h, i: (b, h))
    return pl.pallas_call(
        _dsa_attn_kernel,
        grid=(batch, DSA_HEADS // nh, nq),
        in_specs=(_head_specs(TQ, lambda b, i: b * nq + i, OFF_CQ) + [
            kv_spec, kv_spec,
            pl.BlockSpec((nh, 2, TQ, TK), lambda b, h, i: (h, 0, 0, 0)),
            pl.BlockSpec((1, seq // TK, TQ, TK), lambda b, h, i: (b, 0, i, 0)),
        ]),
        out_specs=pl.BlockSpec((TQ, w), lambda b, h, i: (b * nq + i, h)),
        out_shape=jax.ShapeDtypeStruct((batch * seq, DSA_HEADS * HEAD_DIM), jnp.bfloat16),
        scratch_shapes=[pltpu.VMEM((nh, TQ, LANES), jnp.float32),
                        pltpu.VMEM((nh, TQ, HEAD_DIM + LANES), jnp.float32)],
        compiler_params=_params("parallel", "parallel", "arbitrary"),
        name="dsa_attention",
    )(*([proj] * nh), k_dsa, v_dsa, bias_tiles, mask)


def _mixers(proj, tail, bias_tiles, diff_lambda, diff_subln_g, kv_norm_g, w_uk, w_uv,
            layer_idx, batch, seq):
    lambda_init = 0.8 - 0.6 * math.exp(-0.3 * layer_idx)
    c0 = 2 * DIFF_HEADS
    c1 = c0 + MOBA_HEADS
    flat = lambda w: w.transpose(1, 0, 2).reshape(KV_LATENT, DSA_HEADS * HEAD_DIM).astype(jnp.bfloat16)
    y_diff = _diff_attention(proj, bias_tiles, diff_lambda, diff_subln_g, lambda_init, batch, seq)
    y_moba = _moba_attention(proj, bias_tiles[c0:c1], batch, seq)
    k_dsa, v_dsa, ik_lo, ik_hi, iw = _dsa_prep(tail, kv_norm_g, flat(w_uk), flat(w_uv))
    mask = _dsa_select(proj, ik_lo, ik_hi, iw, batch, seq)
    y_dsa = _dsa_attention(proj, k_dsa, v_dsa, mask, bias_tiles[c1:], batch, seq)
    return [y_diff, y_moba, y_dsa]


def _tail_weights(w_in_b):
    pad = jnp.zeros(w_in_b.shape[:2] + (TAIL_W - (D_IN - OFF_IK) - KV_LATENT,), w_in_b.dtype)
    return jnp.concatenate([w_in_b[:, :, OFF_CKV:OFF_IQ], w_in_b[:, :, OFF_IK:], pad], axis=2)


def kernel(x, ln_emb_g, ln_emb_b, rel_bias, w_in, diff_lambda, diff_subln_g, kv_norm_g, w_uk, w_uv, w_o,
           ln1_g, ln1_b, w_up, w_down, ln2_g, ln2_b):
    batch, seq, d_model = x.shape
    depth = w_in.shape[0]
    alpha = (2.0 * depth) ** 0.25
    bf16 = jnp.bfloat16
    bias_tiles = _bias_tiles(rel_bias)
    w_down_b = w_down.astype(bf16)
    w_in_b = w_in.astype(bf16)
    w_tail_b = _tail_weights(w_in_b)
    h, hb = _layer_norm(x.reshape(batch * seq, d_model), ln_emb_g, ln_emb_b)
    for l in range(depth):
        proj = _matmul_bf16w(hb, w_in_b, l, N_MAIN, bf16)
        tail = _matmul_bf16w(hb, w_tail_b, l, TAIL_W, jnp.float32, tn=TAIL_W)
        mix = _mixers(proj, tail, bias_tiles, diff_lambda[l], diff_subln_g[l], kv_norm_g[l],
                      w_uk[l], w_uv[l], l, batch, seq)
        z = _matmul_f32w(mix, w_o, l, d_model, jnp.float32, residual=h, res_scale=alpha)
        h, hb = _layer_norm(z, ln1_g[l], ln1_b[l])
        up = _matmul_f32w([hb], w_up, l, w_up.shape[2], bf16, relu2=True)
        z = _matmul_acc(up, w_down_b, l, h, alpha)
        h, hb = _layer_norm(z, ln2_g[l], ln2_b[l])
    return h.reshape(batch, seq, d_model)
```
